```python
import math
import jax, jax.numpy as jnp
from jax import lax
import numpy as np

D_MODEL = 1024
BATCH = 8
SEQ = 4096
DEPTH = 1

D_CONV = D_MODEL
CONV_WIDTH = 3
N_HEADS = 16
HEAD_DIM = 64
D_ATTN = N_HEADS * HEAD_DIM
ATTN_PATTERNS = ((128, 1), (512, 4), (2048, 16))
QB = 128
IN_COLS = 4 * D_CONV + 4 * D_ATTN + 2 * D_MODEL
EPS = 1e-6

kernel_name = "hybrid_gated_shortconv_dilated_alibi_attn"


def rms_norm(x, g):
    xf = x.astype(jnp.float32)
    y = xf * lax.rsqrt(jnp.mean(xf * xf, axis=-1, keepdims=True) + EPS)
    return (y * g.astype(jnp.float32)).astype(x.dtype)


def alibi_slopes(n_heads):
    return jnp.exp2(-8.0 * jnp.arange(1, n_heads + 1, dtype=jnp.float32) / n_heads)


def causal_depthwise_conv(u, w):
    return lax.conv_general_dilated(
        u, w.astype(u.dtype)[:, None, :], window_strides=(1,),
        padding=[(CONV_WIDTH - 1, 0)],
        dimension_numbers=("NWC", "WIO", "NWC"),
        feature_group_count=u.shape[-1])


def dilated_window_attention(q, k, v, window, dilation, slopes):
    B, S, H, Dh = q.shape
    d = dilation
    win = window // dilation
    assert win == QB
    L = S // d
    nb = -(-L // QB)
    Lp = nb * QB

    def to_sub(t):
        t = t.reshape(B, L, d, H, Dh)
        return jnp.pad(t, ((0, 0), (0, Lp - L), (0, 0), (0, 0), (0, 0)))

    qs = (to_sub(q) * (Dh ** -0.5)).reshape(B, nb, QB, d, H, Dh)

    def key_blocks(t):
        tp = jnp.pad(to_sub(t), ((0, 0), (QB, 0), (0, 0), (0, 0), (0, 0)))
        prev = tp[:, :Lp].reshape(B, nb, QB, d, H, Dh)
        cur = tp[:, QB:QB + Lp].reshape(B, nb, QB, d, H, Dh)
        return jnp.concatenate([prev, cur], axis=2)

    kb = key_blocks(k)
    vb = key_blocks(v)

    s = jnp.einsum("bnqrhe,bnkrhe->bnrhqk", qs, kb,
                   preferred_element_type=jnp.float32)
    q_loc = jnp.arange(QB)[:, None] + QB
    k_loc = jnp.arange(2 * QB)[None, :]
    delta = q_loc - k_loc
    key_sub = jnp.arange(nb)[:, None] * QB - QB + jnp.arange(2 * QB)[None, :]
    valid = ((delta >= 0) & (delta <= win))[None] & (key_sub >= 0)[:, None, :]
    bias = -slopes[:, None, None] * (d * delta).astype(jnp.float32)[None]
    s = jnp.where(valid[None, :, None, None], s + bias, -jnp.inf)

    m = jnp.max(s, axis=-1, keepdims=True)
    p = jnp.exp(s - m)
    den = jnp.sum(p, axis=-1)
    lse = m[..., 0] + jnp.log(den)
    o = jnp.einsum("bnrhqk,bnkrhe->bnqrhe", p, vb.astype(jnp.float32))
    o = o / jnp.transpose(den, (0, 1, 4, 2, 3))[..., None]
    o = o.reshape(B, Lp, d, H, Dh)[:, :L].reshape(B, S, H, Dh)
    lse = jnp.transpose(lse, (0, 1, 4, 2, 3)).reshape(B, Lp, d, H)[:, :L].reshape(B, S, H)
    return o, lse


def mixture_of_dilations(q, k, v, slopes):
    outs, lses = [], []
    for window, dilation in ATTN_PATTERNS:
        o, lse = dilated_window_attention(q, k, v, window, dilation, slopes)
        outs.append(o)
        lses.append(lse)
    alpha = jax.nn.softmax(jnp.stack(lses, axis=0), axis=0)
    o = jnp.sum(alpha[..., None] * jnp.stack(outs, axis=0), axis=0)
    return o.astype(q.dtype)


def hybrid_mixer(h, w_in, b_merge, conv_w, w_out_conv, w_out_attn, w_o):
    B, S, _ = h.shape
    proj = jnp.einsum("bsd,dc->bsc", h, w_in)
    splits = np.cumsum([D_CONV] * 4 + [D_ATTN] * 4 + [D_MODEL])
    xc, bg, cg, zc, q, k, v, za, g_conv, g_attn = jnp.split(proj, splits, axis=-1)

    c = causal_depthwise_conv(cg * xc, conv_w)
    y_conv = jnp.einsum("bsc,cd->bsd", jax.nn.silu(zc) * bg * c, w_out_conv)

    shp = (B, S, N_HEADS, HEAD_DIM)
    o = mixture_of_dilations(q.reshape(shp), k.reshape(shp), v.reshape(shp),
                             alibi_slopes(N_HEADS)).reshape(B, S, D_ATTN)
    y_attn = jnp.einsum("bsc,cd->bsd", jax.nn.silu(za) * o, w_out_attn)

    g_conv = jax.nn.sigmoid(g_conv + b_merge[:D_MODEL])
    g_attn = jax.nn.sigmoid(g_attn + b_merge[D_MODEL:])
    merged = g_conv * y_conv + g_attn * y_attn
    return jnp.einsum("bsd,de->bse", merged, w_o)


def setup_inputs(seed: int = 0) -> dict:
    key = jax.random.key(seed)
    ks = jax.random.split(key, 10)
    f32 = jnp.float32
    x = jax.random.normal(ks[0], (BATCH, SEQ, D_MODEL), f32)
    norm_g = 1.0 + 0.02 * jax.random.normal(ks[1], (DEPTH, D_MODEL), f32)
    w_in = jax.random.normal(ks[2], (DEPTH, D_MODEL, IN_COLS), f32) * D_MODEL ** -0.5
    b_merge = 0.01 * jax.random.normal(ks[3], (DEPTH, 2 * D_MODEL), f32)
    conv_w = jax.random.normal(ks[4], (DEPTH, CONV_WIDTH, D_CONV), f32) * CONV_WIDTH ** -0.5
    w_out_conv = jax.random.normal(ks[5], (DEPTH, D_CONV, D_MODEL), f32) * D_CONV ** -0.5
    w_out_attn = jax.random.normal(ks[6], (DEPTH, D_ATTN, D_MODEL), f32) * D_ATTN ** -0.5
    w_o = jax.random.normal(ks[7], (DEPTH, D_MODEL, D_MODEL), f32) * D_MODEL ** -0.5
    final_g = 1.0 + 0.02 * jax.random.normal(ks[8], (D_MODEL,), f32)
    return {"x": x, "norm_g": norm_g, "w_in": w_in, "b_merge": b_merge,
            "conv_w": conv_w, "w_out_conv": w_out_conv, "w_out_attn": w_out_attn,
            "w_o": w_o, "final_g": final_g}


def reference(x, norm_g, w_in, b_merge, conv_w, w_out_conv, w_out_attn, w_o, final_g):
    h = x
    for layer in range(DEPTH):
        u = rms_norm(h, norm_g[layer])
        h = h + hybrid_mixer(u, w_in[layer], b_merge[layer], conv_w[layer],
                             w_out_conv[layer], w_out_attn[layer], w_o[layer])
    return rms_norm(h, final_g)
```

```python
import functools

import jax
import jax.numpy as jnp
from jax import lax
from jax.experimental import pallas as pl
from jax.experimental.pallas import tpu as pltpu

N_HEADS = 16
HEAD_DIM = 64
CONV_WIDTH = 3
ATTN_PATTERNS = ((128, 1), (512, 4), (2048, 16))
QB = 128
EPS = 1e-6

LANES = 128
HEADS_PER_LANE_TILE = LANES // HEAD_DIM
PROJ_ROWS = 256
OUT_ROWS = 512
VMEM_LIMIT_BYTES = 56 * 1024 * 1024

BF16 = jnp.bfloat16
F32 = jnp.float32


def _rms_norm(x, g):
    return x * lax.rsqrt(jnp.mean(x * x, axis=-1, keepdims=True) + EPS) * g


def _proj_kernel(x_ref, g_ref, w_ref, b_ref, cw_ref,
                 q_ref, k_ref, v_ref, za_ref, ac_ref, gc_ref, ga_ref, abuf):
    rows, d = x_ref.shape
    u = _rms_norm(x_ref[...], g_ref[...]).astype(BF16)

    def proj(c):
        return jnp.dot(u, w_ref[:, c * d:(c + 1) * d], preferred_element_type=F32)

    @pl.when(pl.program_id(1) == 0)
    def _():
        abuf[0:8, :] = jnp.zeros((8, d), F32)

    a = proj(2) * proj(0)
    abuf[8:8 + rows, :] = a
    c = (cw_ref[2:3, :] * a
         + cw_ref[1:2, :] * abuf[7:7 + rows, :]
         + cw_ref[0:1, :] * abuf[6:6 + rows, :])
    abuf[0:8, :] = abuf[rows:rows + 8, :]
    ac_ref[...] = (jax.nn.silu(proj(3)) * proj(1) * c).astype(BF16)

    q_ref[...] = (proj(4) * (HEAD_DIM ** -0.5)).astype(BF16)
    k_ref[...] = proj(5).astype(BF16)
    v_ref[...] = proj(6).astype(BF16)
    za_ref[...] = jax.nn.silu(proj(7)).astype(BF16)
    gc_ref[...] = jax.nn.sigmoid(proj(8) + b_ref[:, 0:d]).astype(BF16)
    ga_ref[...] = jax.nn.sigmoid(proj(9) + b_ref[:, d:2 * d]).astype(BF16)


def _proj_stage(x, norm_g, w_in, b_merge, conv_w):
    bsz, seq, d = x.shape
    rows = PROJ_ROWS
    tile = pl.BlockSpec((None, rows, d), lambda b, i: (b, i, 0))
    const = lambda shape: pl.BlockSpec(shape, lambda b, i: (0,) * len(shape),
                                       pipeline_mode=pl.Buffered(1))
    out = jax.ShapeDtypeStruct((bsz, seq, d), BF16)
    return pl.pallas_call(
        _proj_kernel,
        grid=(bsz, seq // rows),
        in_specs=[tile, const((1, d)), const(w_in.shape), const((1, 2 * d)),
                  const((CONV_WIDTH, d))],
        out_specs=[tile] * 7,
        out_shape=[out] * 7,
        scratch_shapes=[pltpu.VMEM((rows + 8, d), F32)],
        compiler_params=pltpu.CompilerParams(
            dimension_semantics=("arbitrary", "arbitrary"),
            vmem_limit_bytes=VMEM_LIMIT_BYTES),
        name="proj_stage",
    )(x, norm_g, w_in, b_merge, conv_w)


def _attn_kernel(*refs, merge_prev):
    if merge_prev:
        (bias_ref, q_ref, kp_ref, kc_ref, vp_ref, vc_ref, op_ref, lp_ref,
         o_ref, lse_ref) = refs
    else:
        bias_ref, q_ref, kp_ref, kc_ref, vp_ref, vc_ref, o_ref, lse_ref = refs
    first = (pl.program_id(2) == 0).astype(jnp.int32)
    lane = lax.broadcasted_iota(jnp.int32, (QB, LANES), 1)
    head_masks = [(lane // HEAD_DIM) == e for e in range(HEADS_PER_LANE_TILE)]
    lse_tile = jnp.zeros((QB, LANES), F32)
    lse_prev = lp_ref[...] if merge_prev else None

    for t in range(N_HEADS // HEADS_PER_LANE_TILE):
        cols = slice(t * LANES, (t + 1) * LANES)
        q = q_ref[:, cols]
        k = jnp.concatenate([kp_ref[:, cols], kc_ref[:, cols]], axis=0)
        v = jnp.concatenate([vp_ref[:, cols], vc_ref[:, cols]], axis=0)
        q_heads = jnp.concatenate(
            [jnp.where(m, q, jnp.zeros_like(q)) for m in head_masks], axis=0)
        s_all = lax.dot_general(q_heads, k, (((1,), (1,)), ((), ())),
                                preferred_element_type=F32)
        ps, ms, ls = [], [], []
        for e in range(HEADS_PER_LANE_TILE):
            h = t * HEADS_PER_LANE_TILE + e
            s = s_all[e * QB:(e + 1) * QB, :] + bias_ref[first, h]
            m = jnp.max(s, axis=-1, keepdims=True)
            p = jnp.exp(s - m)
            ls.append(jnp.sum(p, axis=-1, keepdims=True))
            ms.append(m)
            ps.append(p.astype(BF16))
        acc_all = jnp.dot(jnp.concatenate(ps, axis=0), v, preferred_element_type=F32)
        o_tile = jnp.zeros((QB, LANES), F32)
        for e in range(HEADS_PER_LANE_TILE):
            h = t * HEADS_PER_LANE_TILE + e
            acc = acc_all[e * QB:(e + 1) * QB, :]
            if merge_prev:
                lp = lse_prev[:, h:h + 1]
                top = jnp.maximum(ms[e], lp)
                w_cur = jnp.exp(ms[e] - top)
                w_prev = jnp.exp(lp - top)
                den = w_cur * ls[e] + w_prev
                o_h = (w_cur * acc + w_prev * op_ref[:, cols].astype(F32)) / den
                lse_h = top + jnp.log(den)
            else:
                o_h = acc / ls[e]
                lse_h = ms[e] + jnp.log(ls[e])
            o_tile = jnp.where(head_masks[e], o_h, o_tile)
            lse_tile = jnp.where(lane == h, lse_h, lse_tile)
        o_ref[:, cols] = o_tile.astype(o_ref.dtype)
    lse_ref[...] = lse_tile


def _alibi_bias(dilation):
    slopes = jnp.exp2(-8.0 * jnp.arange(1, N_HEADS + 1, dtype=F32) / N_HEADS)
    delta = (jnp.arange(QB)[:, None] + QB) - jnp.arange(2 * QB)[None, :]
    valid = (delta >= 0) & (delta <= QB)
    has_prev = jnp.stack([jnp.ones((2 * QB,), bool), jnp.arange(2 * QB) >= QB])
    bias = -slopes[:, None, None] * (dilation * delta).astype(F32)[None]
    ok = valid[None, None] & has_prev[:, None, None, :]
    return jnp.where(ok, bias[None], -jnp.inf)


def _attn_stage(q, k, v, dilation, prev):
    bsz, seq, d = q.shape
    sub = seq // dilation
    nb = sub // QB
    view = lambda a: a.reshape(bsz, sub, dilation * a.shape[-1])
    cur = lambda w: pl.BlockSpec((None, QB, w), lambda b, r, i: (b, i, r))
    prv = lambda w: pl.BlockSpec((None, QB, w), lambda b, r, i: (b, jnp.maximum(i - 1, 0), r))
    bias = _alibi_bias(dilation)
    bias_spec = pl.BlockSpec(bias.shape, lambda b, r, i: (0, 0, 0, 0),
                             pipeline_mode=pl.Buffered(1))
    in_specs = [bias_spec, cur(d), prv(d), cur(d), prv(d), cur(d)]
    args = [bias, view(q), view(k), view(k), view(v), view(v)]
    if prev is not None:
        in_specs += [cur(d), cur(LANES)]
        args += [view(prev[0]), view(prev[1])]
    o, lse = pl.pallas_call(
        functools.partial(_attn_kernel, merge_prev=prev is not None),
        grid=(bsz, dilation, nb),
        in_specs=in_specs,
        out_specs=[cur(d), cur(LANES)],
        out_shape=[jax.ShapeDtypeStruct((bsz, sub, dilation * d), BF16),
                   jax.ShapeDtypeStruct((bsz, sub, dilation * LANES), F32)],
        compiler_params=pltpu.CompilerParams(
            dimension_semantics=("arbitrary", "arbitrary", "arbitrary"),
            vmem_limit_bytes=VMEM_LIMIT_BYTES),
        name=f"attn_stage_d{dilation}",
    )(*args)
    return o.reshape(bsz, seq, d), lse.reshape(bsz, seq, LANES)


def _out_kernel(x_ref, ac_ref, o_ref, za_ref, gc_ref, ga_ref,
                woc_ref, woa_ref, wo_ref, fg_ref, y_ref):
    y_conv = jnp.dot(ac_ref[...], woc_ref[...], preferred_element_type=F32)
    gated_o = (za_ref[...].astype(F32) * o_ref[...].astype(F32)).astype(BF16)
    y_attn = jnp.dot(gated_o, woa_ref[...], preferred_element_type=F32)
    merged = gc_ref[...].astype(F32) * y_conv + ga_ref[...].astype(F32) * y_attn
    mixed = jnp.dot(merged.astype(BF16), wo_ref[...], preferred_element_type=F32)
    y_ref[...] = _rms_norm(x_ref[...] + mixed, fg_ref[...])


def _out_stage(x, ac, o, za, gc, ga, w_out_conv, w_out_attn, w_o, final_g):
    n, d = x.shape
    rows = OUT_ROWS
    tile = pl.BlockSpec((rows, d), lambda i: (i, 0))
    const = lambda shape: pl.BlockSpec(shape, lambda i: (0,) * len(shape),
                                       pipeline_mode=pl.Buffered(1))
    return pl.pallas_call(
        _out_kernel,
        grid=(n // rows,),
        in_specs=[tile] * 6 + [const((d, d))] * 3 + [const((1, d))],
        out_specs=tile,
        out_shape=jax.ShapeDtypeStruct((n, d), F32),
        compiler_params=pltpu.CompilerParams(
            dimension_semantics=("arbitrary",),
            vmem_limit_bytes=VMEM_LIMIT_BYTES),
        name="out_stage",
    )(x, ac, o, za, gc, ga, w_out_conv, w_out_attn, w_o, final_g)


def _layer(h, norm_g, w_in, b_merge, conv_w, w_out_conv, w_out_attn, w_o, out_g):
    bsz, seq, d = h.shape
    q, k, v, za, ac, gc, ga = _proj_stage(
        h, norm_g[None, :], w_in.astype(BF16), b_merge[None, :], conv_w)
    prev = None
    for _, dilation in ATTN_PATTERNS:
        prev = _attn_stage(q, k, v, dilation, prev)
    flat = lambda a: a.reshape(bsz * seq, d)
    y = _out_stage(flat(h), flat(ac), flat(prev[0]), flat(za), flat(gc), flat(ga),
                   w_out_conv.astype(BF16), w_out_attn.astype(BF16), w_o.astype(BF16),
                   out_g[None, :])
    return y.reshape(bsz, seq, d)


def kernel(x, norm_g, w_in, b_merge, conv_w, w_out_conv, w_out_attn, w_o, final_g):
    depth = norm_g.shape[0]
    assert depth == 1, "the fused output stage applies the final norm after a single layer"
    assert x.shape[-1] == N_HEADS * HEAD_DIM
    assert all(x.shape[1] % (QB * dil) == 0 and win == QB * dil for win, dil in ATTN_PATTERNS)
    return _layer(x, norm_g[0], w_in[0], b_merge[0], conv_w[0],
                  w_out_conv[0], w_out_attn[0], w_o[0], final_g)
```

```python
import functools

import jax
import jax.numpy as jnp
import numpy as np
from jax import lax
from jax.experimental import pallas as pl
from jax.experimental.pallas import tpu as pltpu

N_HEADS = 16
HEAD_DIM = 64
CONV_WIDTH = 3
ATTN_PATTERNS = ((128, 1), (512, 4), (2048, 16))
QB = 128
EPS = 1e-6

LANES = 128
SUBLANES = 8
HEADS_PER_LANE_TILE = LANES // HEAD_DIM
GROUPS = QB // SUBLANES
PROJ_ROWS = 256
OUT_ROWS = 256
VMEM_LIMIT_BYTES = 56 * 1024 * 1024

BF16 = jnp.bfloat16
F32 = jnp.float32


def _rms_norm(x, g):
    return x * lax.rsqrt(jnp.mean(x * x, axis=-1, keepdims=True) + EPS) * g


def _to_storage_order(rows):
    s = np.arange(rows)
    natural = (s // QB) * QB + (s % SUBLANES) * GROUPS + (s % QB) // SUBLANES
    p = np.zeros((rows, rows), np.float32)
    p[s, natural] = 1.0
    return jnp.asarray(p, BF16)


def _permute_rows(p, a):
    return jnp.dot(p, a, preferred_element_type=F32).astype(BF16)


def _previous_token(a, carry_group):
    rows, d = a.shape
    first_row = lax.broadcasted_iota(jnp.int32, (SUBLANES, d), 0) == 0
    pieces = []
    prev_last = carry_group
    for blk in range(rows // QB):
        base = blk * QB
        last = a[base + QB - SUBLANES:base + QB, :]
        pieces.append(jnp.where(first_row, pltpu.roll(prev_last, 1, 0), pltpu.roll(last, 1, 0)))
        pieces.append(a[base:base + QB - SUBLANES, :])
        prev_last = last
    return jnp.concatenate(pieces, axis=0)


def _proj_kernel(x_ref, perm_ref, g_ref, w_ref, b_ref, cw_ref,
                 q_ref, k_ref, v_ref, za_ref, ac_ref, gc_ref, ga_ref, carry):
    rows, d = x_ref.shape
    u = _permute_rows(perm_ref[...], _rms_norm(x_ref[...], g_ref[...]).astype(BF16))

    def proj(c):
        return jnp.dot(u, w_ref[:, c * d:(c + 1) * d], preferred_element_type=F32)

    @pl.when(pl.program_id(1) == 0)
    def _():
        carry[...] = jnp.zeros(carry.shape, F32)

    a = proj(2) * proj(0)
    back1 = _previous_token(a, carry[SUBLANES:2 * SUBLANES, :])
    back2 = _previous_token(back1, carry[0:SUBLANES, :])
    carry[...] = a[rows - 2 * SUBLANES:rows, :]
    c = cw_ref[2:3, :] * a + cw_ref[1:2, :] * back1 + cw_ref[0:1, :] * back2
    ac_ref[...] = (jax.nn.silu(proj(3)) * proj(1) * c).astype(BF16)

    q_ref[...] = proj(4) * (HEAD_DIM ** -0.5)
    k_ref[...] = proj(5)
    v_ref[...] = proj(6)
    za_ref[...] = jax.nn.silu(proj(7)).astype(BF16)
    gc_ref[...] = jax.nn.sigmoid(proj(8) + b_ref[:, 0:d]).astype(BF16)
    ga_ref[...] = jax.nn.sigmoid(proj(9) + b_ref[:, d:2 * d]).astype(BF16)


def _proj_stage(x, norm_g, w_in, b_merge, conv_w):
    bsz, seq, d = x.shape
    rows = PROJ_ROWS
    tile = pl.BlockSpec((None, rows, d), lambda b, i: (b, i, 0))
    const = lambda shape: pl.BlockSpec(shape, lambda b, i: (0,) * len(shape),
                                       pipeline_mode=pl.Buffered(1))
    wide = jax.ShapeDtypeStruct((bsz, seq, d), F32)
    narrow = jax.ShapeDtypeStruct((bsz, seq, d), BF16)
    return pl.pallas_call(
        _proj_kernel,
        grid=(bsz, seq // rows),
        in_specs=[tile, const((rows, rows)), const((1, d)), const(w_in.shape),
                  const((1, 2 * d)), const((CONV_WIDTH, d))],
        out_specs=[tile] * 7,
        out_shape=[wide] * 3 + [narrow] * 4,
        scratch_shapes=[pltpu.VMEM((2 * SUBLANES, d), F32)],
        compiler_params=pltpu.CompilerParams(
            dimension_semantics=("arbitrary", "arbitrary"),
            vmem_limit_bytes=VMEM_LIMIT_BYTES),
        name="proj_stage",
    )(x, _to_storage_order(rows), norm_g, w_in, b_merge, conv_w)


def _attn_kernel(bias_ref, q_ref, k_ref, v_ref, o_ref, lse_ref, kbuf, vbuf):
    d = q_ref.shape[-1]
    step = pl.program_id(2)
    slot = step % 2
    first = (step == 0).astype(jnp.int32)

    @pl.when(step == 0)
    def _():
        kbuf[1] = jnp.zeros((QB, d), BF16)
        vbuf[1] = jnp.zeros((QB, d), BF16)

    kbuf[slot] = k_ref[...].reshape(QB, d).astype(BF16)
    vbuf[slot] = v_ref[...].reshape(QB, d).astype(BF16)

    lane = lax.broadcasted_iota(jnp.int32, (QB, LANES), 1)
    head_masks = [(lane // HEAD_DIM) == e for e in range(HEADS_PER_LANE_TILE)]
    lse_tile = jnp.zeros((QB, LANES), F32)

    for t in range(N_HEADS // HEADS_PER_LANE_TILE):
        cols = slice(t * LANES, (t + 1) * LANES)
        q = q_ref[:, :, :, cols].reshape(QB, LANES).astype(BF16)
        k = jnp.concatenate([kbuf[1 - slot, :, cols], kbuf[slot, :, cols]], axis=0)
        v = jnp.concatenate([vbuf[1 - slot, :, cols], vbuf[slot, :, cols]], axis=0)
        q_heads = jnp.concatenate(
            [jnp.where(m, q, jnp.zeros_like(q)) for m in head_masks], axis=0)
        s_all = lax.dot_general(q_heads, k, (((1,), (1,)), ((), ())),
                                preferred_element_type=F32)
        ps, ms, ls = [], [], []
        for e in range(HEADS_PER_LANE_TILE):
            h = t * HEADS_PER_LANE_TILE + e
            s = s_all[e * QB:(e + 1) * QB, :] + bias_ref[first, h]
            m = jnp.max(s, axis=-1, keepdims=True)
            p = jnp.exp(s - m)
            ls.append(jnp.sum(p, axis=-1, keepdims=True))
            ms.append(m)
            ps.append(p.astype(BF16))
        acc_all = jnp.dot(jnp.concatenate(ps, axis=0), v, preferred_element_type=F32)
        o_tile = jnp.zeros((QB, LANES), F32)
        for e in range(HEADS_PER_LANE_TILE):
            h = t * HEADS_PER_LANE_TILE + e
            o_tile = jnp.where(head_masks[e], acc_all[e * QB:(e + 1) * QB, :] / ls[e], o_tile)
            lse_tile = jnp.where(lane == h, ms[e] + jnp.log(ls[e]), lse_tile)
        o_ref[:, :, :, cols] = o_tile.reshape(o_ref.shape[:-1] + (LANES,))
    lse_ref[...] = lse_tile.reshape(lse_ref.shape)


def _block_positions(dilation):
    row = np.arange(QB)
    if dilation == 16:
        return row
    if dilation == 4:
        return (row // 32) * 32 + (row % 8) * 4 + (row // 8) % 4
    if dilation == 1:
        return (row % 8) * 16 + row // 8
    raise NotImplementedError(dilation)


def _alibi_bias(dilation):
    pos = _block_positions(dilation)
    q_loc = pos[:, None] + QB
    k_loc = np.concatenate([pos, pos + QB])[None, :]
    delta = q_loc - k_loc
    valid = (delta >= 0) & (delta <= QB)
    has_prev = np.stack([np.ones((2 * QB,), bool), np.arange(2 * QB) >= QB])
    ok = jnp.asarray(valid[None, None] & has_prev[:, None, None, :])
    slopes = jnp.exp2(-8.0 * jnp.arange(1, N_HEADS + 1, dtype=F32) / N_HEADS)
    bias = -slopes[:, None, None] * jnp.asarray(dilation * delta, F32)[None]
    return jnp.where(ok, bias[None], -jnp.inf)


def _box_spec(dilation, seq, width):
    if dilation == 16:
        return pl.BlockSpec((None, None, 4, 4, None, None, SUBLANES, width),
                            lambda b, r, i: (b, i, 0, 0, r // 4, r % 4, 0, 0))
    if dilation == 4:
        return pl.BlockSpec((None, None, None, 4, 4, None, SUBLANES, width),
                            lambda b, r, i: (b, i // 4, i % 4, 0, 0, r, 0, 0))
    if dilation == 1:
        return pl.BlockSpec((None, None, None, None, 4, 4, SUBLANES, width),
                            lambda b, r, i: (b, i // 16, (i // 4) % 4, i % 4, 0, 0, 0, 0))
    raise NotImplementedError(dilation)


def _attn_stage(q, k, v, dilation):
    bsz, seq, d = q.shape
    nb = seq // (dilation * QB)
    view = lambda a: a.reshape(bsz, seq // 2048, 4, 4, 4, 4, SUBLANES, a.shape[-1])
    bias = _alibi_bias(dilation)
    bias_spec = pl.BlockSpec(bias.shape, lambda b, r, i: (0, 0, 0, 0),
                             pipeline_mode=pl.Buffered(1))
    box = _box_spec(dilation, seq, d)
    o, lse = pl.pallas_call(
        _attn_kernel,
        grid=(bsz, dilation, nb),
        in_specs=[bias_spec, box, box, box],
        out_specs=[box, _box_spec(dilation, seq, LANES)],
        out_shape=[jax.ShapeDtypeStruct(view(q).shape, F32),
                   jax.ShapeDtypeStruct(view(q).shape[:-1] + (LANES,), F32)],
        scratch_shapes=[pltpu.VMEM((2, QB, d), BF16), pltpu.VMEM((2, QB, d), BF16)],
        compiler_params=pltpu.CompilerParams(
            dimension_semantics=("arbitrary", "arbitrary", "arbitrary"),
            vmem_limit_bytes=VMEM_LIMIT_BYTES),
        name=f"attn_stage_d{dilation}",
    )(bias, view(q), view(k), view(v))
    return o.reshape(bsz * seq, d), lse.reshape(bsz * seq, LANES)


def _out_kernel(*refs, n_patterns):
    x_ref, ac_ref, za_ref, gc_ref, ga_ref = refs[:5]
    o_refs = refs[5:5 + n_patterns]
    lse_refs = refs[5 + n_patterns:5 + 2 * n_patterns]
    perm_ref, woc_ref, woa_ref, wo_ref, fg_ref, y_ref = refs[5 + 2 * n_patterns:]
    rows, d = x_ref.shape

    lses = [r[...] for r in lse_refs]
    top = functools.reduce(jnp.maximum, lses)
    ws = [jnp.exp(l - top) for l in lses]
    total = functools.reduce(jnp.add, ws)
    alphas = [w / total for w in ws]

    lane = lax.broadcasted_iota(jnp.int32, (rows, LANES), 1)
    head_masks = [(lane // HEAD_DIM) == e for e in range(HEADS_PER_LANE_TILE)]
    gated = []
    for t in range(N_HEADS // HEADS_PER_LANE_TILE):
        cols = slice(t * LANES, (t + 1) * LANES)
        o_tile = jnp.zeros((rows, LANES), F32)
        for alpha, o_ref in zip(alphas, o_refs):
            w_tile = jnp.zeros((rows, LANES), F32)
            for e in range(HEADS_PER_LANE_TILE):
                h = t * HEADS_PER_LANE_TILE + e
                w_tile = jnp.where(head_masks[e], alpha[:, h:h + 1], w_tile)
            o_tile = o_tile + w_tile * o_ref[:, cols]
        gated.append((za_ref[:, cols].astype(F32) * o_tile).astype(BF16))
    y_attn = jnp.dot(jnp.concatenate(gated, axis=1), woa_ref[...], preferred_element_type=F32)
    y_conv = jnp.dot(ac_ref[...], woc_ref[...], preferred_element_type=F32)
    merged = gc_ref[...].astype(F32) * y_conv + ga_ref[...].astype(F32) * y_attn
    merged = _permute_rows(perm_ref[...], merged.astype(BF16))
    mixed = jnp.dot(merged, wo_ref[...], preferred_element_type=F32)
    y_ref[...] = _rms_norm(x_ref[...] + mixed, fg_ref[...])


def _out_stage(x, ac, za, gc, ga, os_, lses, w_out_conv, w_out_attn, w_o, final_g):
    n, d = x.shape
    rows = OUT_ROWS
    n_patterns = len(os_)
    tile = pl.BlockSpec((rows, d), lambda i: (i, 0))
    stat = pl.BlockSpec((rows, LANES), lambda i: (i, 0))
    const = lambda shape: pl.BlockSpec(shape, lambda i: (0,) * len(shape),
                                       pipeline_mode=pl.Buffered(1))
    return pl.pallas_call(
        functools.partial(_out_kernel, n_patterns=n_patterns),
        grid=(n // rows,),
        in_specs=([tile] * (5 + n_patterns) + [stat] * n_patterns + [const((rows, rows))]
                  + [const((d, d))] * 3 + [const((1, d))]),
        out_specs=tile,
        out_shape=jax.ShapeDtypeStruct((n, d), F32),
        compiler_params=pltpu.CompilerParams(
            dimension_semantics=("arbitrary",),
            vmem_limit_bytes=VMEM_LIMIT_BYTES),
        name="out_stage",
    )(x, ac, za, gc, ga, *os_, *lses, _to_storage_order(rows).T,
      w_out_conv, w_out_attn, w_o, final_g)


def _layer(h, norm_g, w_in, b_merge, conv_w, w_out_conv, w_out_attn, w_o, out_g):
    bsz, seq, d = h.shape
    q, k, v, za, ac, gc, ga = _proj_stage(
        h, norm_g[None, :], w_in.astype(BF16), b_merge[None, :], conv_w)
    os_, lses = zip(*[_attn_stage(q, k, v, dilation) for _, dilation in ATTN_PATTERNS])
    flat = lambda a: a.reshape(bsz * seq, d)
    y = _out_stage(flat(h), flat(ac), flat(za), flat(gc), flat(ga), os_, lses,
                   w_out_conv.astype(BF16), w_out_attn.astype(BF16), w_o.astype(BF16),
                   out_g[None, :])
    return y.reshape(bsz, seq, d)


def kernel(x, norm_g, w_in, b_merge, conv_w, w_out_conv, w_out_attn, w_o, final_g):
    depth = norm_g.shape[0]
    assert depth == 1, "the fused output stage applies the final norm after a single layer"
    assert x.shape[-1] == N_HEADS * HEAD_DIM and x.shape[1] % 2048 == 0
    assert ATTN_PATTERNS == ((128, 1), (512, 4), (2048, 16)), "box specs are per pattern"
    return _layer(x, norm_g[0], w_in[0], b_merge[0], conv_w[0],
                  w_out_conv[0], w_out_attn[0], w_o[0], final_g)
```

```python
import functools

import jax
import jax.numpy as jnp
import numpy as np
from jax import lax
from jax.experimental import pallas as pl
from jax.experimental.pallas import tpu as pltpu

N_HEADS = 16
HEAD_DIM = 64
CONV_WIDTH = 3
ATTN_PATTERNS = ((128, 1), (512, 4), (2048, 16))
QB = 128
EPS = 1e-6

LANES = 128
SUBLANES = 8
HEADS_PER_LANE_TILE = LANES // HEAD_DIM
GROUPS = QB // SUBLANES
ATTN_BLOCKS = 2
PROJ_ROWS = 256
OUT_ROWS = 256
VMEM_LIMIT_BYTES = 56 * 1024 * 1024

BF16 = jnp.bfloat16
F32 = jnp.float32


def _rms_norm(x, g):
    return x * lax.rsqrt(jnp.mean(x * x, axis=-1, keepdims=True) + EPS) * g


def _to_storage_order(rows):
    s = np.arange(rows)
    natural = (s // QB) * QB + (s % SUBLANES) * GROUPS + (s % QB) // SUBLANES
    p = np.zeros((rows, rows), np.float32)
    p[s, natural] = 1.0
    return jnp.asarray(p, BF16)


def _permute_rows(p, a):
    return jnp.dot(p, a, preferred_element_type=F32).astype(BF16)


def _previous_token(a, carry_group):
    rows, d = a.shape
    first_row = lax.broadcasted_iota(jnp.int32, (SUBLANES, d), 0) == 0
    pieces = []
    prev_last = carry_group
    for blk in range(rows // QB):
        base = blk * QB
        last = a[base + QB - SUBLANES:base + QB, :]
        pieces.append(jnp.where(first_row, pltpu.roll(prev_last, 1, 0), pltpu.roll(last, 1, 0)))
        pieces.append(a[base:base + QB - SUBLANES, :])
        prev_last = last
    return jnp.concatenate(pieces, axis=0)


def _proj_kernel(x_ref, perm_ref, g_ref, w_ref, b_ref, cw_ref,
                 q_ref, k_ref, v_ref, za_ref, ac_ref, gc_ref, ga_ref, carry):
    rows, d = x_ref.shape
    u = _permute_rows(perm_ref[...], _rms_norm(x_ref[...], g_ref[...]).astype(BF16))

    def proj(c):
        return jnp.dot(u, w_ref[:, c * d:(c + 1) * d], preferred_element_type=F32)

    @pl.when(pl.program_id(1) == 0)
    def _():
        carry[...] = jnp.zeros(carry.shape, F32)

    a = proj(2) * proj(0)
    back1 = _previous_token(a, carry[SUBLANES:2 * SUBLANES, :])
    back2 = _previous_token(back1, carry[0:SUBLANES, :])
    carry[...] = a[rows - 2 * SUBLANES:rows, :]
    c = cw_ref[2:3, :] * a + cw_ref[1:2, :] * back1 + cw_ref[0:1, :] * back2
    ac_ref[...] = (jax.nn.silu(proj(3)) * proj(1) * c).astype(BF16)

    q_ref[...] = proj(4) * (HEAD_DIM ** -0.5)
    k_ref[...] = proj(5)
    v_ref[...] = proj(6)
    za_ref[...] = jax.nn.silu(proj(7)).astype(BF16)
    gc_ref[...] = jax.nn.sigmoid(proj(8) + b_ref[:, 0:d]).astype(BF16)
    ga_ref[...] = jax.nn.sigmoid(proj(9) + b_ref[:, d:2 * d]).astype(BF16)


def _proj_stage(x, norm_g, w_in, b_merge, conv_w):
    bsz, seq, d = x.shape
    rows = PROJ_ROWS
    tile = pl.BlockSpec((None, rows, d), lambda b, i: (b, i, 0))
    const = lambda shape: pl.BlockSpec(shape, lambda b, i: (0,) * len(shape),
                                       pipeline_mode=pl.Buffered(1))
    wide = jax.ShapeDtypeStruct((bsz, seq, d), F32)
    narrow = jax.ShapeDtypeStruct((bsz, seq, d), BF16)
    return pl.pallas_call(
        _proj_kernel,
        grid=(bsz, seq // rows),
        in_specs=[tile, const((rows, rows)), const((1, d)), const(w_in.shape),
                  const((1, 2 * d)), const((CONV_WIDTH, d))],
        out_specs=[tile] * 7,
        out_shape=[wide] * 3 + [narrow] * 4,
        scratch_shapes=[pltpu.VMEM((2 * SUBLANES, d), F32)],
        compiler_params=pltpu.CompilerParams(
            dimension_semantics=("arbitrary", "arbitrary"),
            vmem_limit_bytes=VMEM_LIMIT_BYTES),
        name="proj_stage",
    )(x, _to_storage_order(rows), norm_g, w_in, b_merge, conv_w)


def _attn_kernel(bias_ref, q_ref, k_ref, v_ref, o_ref, lse_ref, kbuf, vbuf):
    n_blocks, d = q_ref.shape[0], q_ref.shape[-1]
    step = pl.program_id(2)
    first = (step == 0).astype(jnp.int32)

    @pl.when(step == 0)
    def _():
        kbuf[0] = jnp.zeros((QB, d), BF16)
        vbuf[0] = jnp.zeros((QB, d), BF16)

    lane = lax.broadcasted_iota(jnp.int32, (QB, LANES), 1)
    head_masks = [(lane // HEAD_DIM) == e for e in range(HEADS_PER_LANE_TILE)]

    for g in range(n_blocks):
        keep = (g + 1) % n_blocks
        lse_tile = jnp.zeros((QB, LANES), F32)
        for t in range(N_HEADS // HEADS_PER_LANE_TILE):
            cols = slice(t * LANES, (t + 1) * LANES)
            block = lambda ref: ref[g, :, :, :, cols].reshape(QB, LANES).astype(BF16)
            q, k_cur, v_cur = block(q_ref), block(k_ref), block(v_ref)
            k = jnp.concatenate([kbuf[g, :, cols], k_cur], axis=0)
            v = jnp.concatenate([vbuf[g, :, cols], v_cur], axis=0)
            kbuf[keep, :, cols] = k_cur
            vbuf[keep, :, cols] = v_cur
            q_heads = jnp.concatenate(
                [jnp.where(m, q, jnp.zeros_like(q)) for m in head_masks], axis=0)
            s_all = lax.dot_general(q_heads, k, (((1,), (1,)), ((), ())),
                                    preferred_element_type=F32)
            ps, ms, ls = [], [], []
            for e in range(HEADS_PER_LANE_TILE):
                h = t * HEADS_PER_LANE_TILE + e
                s = s_all[e * QB:(e + 1) * QB, :] + bias_ref[first if g == 0 else 0, h]
                m = jnp.max(s, axis=-1, keepdims=True)
                p = jnp.exp(s - m)
                ls.append(jnp.sum(p, axis=-1, keepdims=True))
                ms.append(m)
                ps.append(p.astype(BF16))
            acc_all = jnp.dot(jnp.concatenate(ps, axis=0), v, preferred_element_type=F32)
            o_tile = jnp.zeros((QB, LANES), F32)
            for e in range(HEADS_PER_LANE_TILE):
                h = t * HEADS_PER_LANE_TILE + e
                o_tile = jnp.where(head_masks[e], acc_all[e * QB:(e + 1) * QB, :] / ls[e], o_tile)
                lse_tile = jnp.where(lane == h, ms[e] + jnp.log(ls[e]), lse_tile)
            o_ref[g, :, :, :, cols] = o_tile.reshape(o_ref.shape[1:-1] + (LANES,))
        lse_ref[g] = lse_tile.reshape(lse_ref.shape[1:])


def _block_positions(dilation):
    row = np.arange(QB)
    if dilation == 16:
        return row
    if dilation == 4:
        return (row // 32) * 32 + (row % 8) * 4 + (row // 8) % 4
    if dilation == 1:
        return (row % 8) * 16 + row // 8
    raise NotImplementedError(dilation)


def _alibi_bias(dilation):
    pos = _block_positions(dilation)
    q_loc = pos[:, None] + QB
    k_loc = np.concatenate([pos, pos + QB])[None, :]
    delta = q_loc - k_loc
    valid = (delta >= 0) & (delta <= QB)
    has_prev = np.stack([np.ones((2 * QB,), bool), np.arange(2 * QB) >= QB])
    ok = jnp.asarray(valid[None, None] & has_prev[:, None, None, :])
    slopes = jnp.exp2(-8.0 * jnp.arange(1, N_HEADS + 1, dtype=F32) / N_HEADS)
    bias = -slopes[:, None, None] * jnp.asarray(dilation * delta, F32)[None]
    return jnp.where(ok, bias[None], -jnp.inf)


def _box_spec(dilation, width):
    g = ATTN_BLOCKS
    per = 4 // g
    if dilation == 16:
        return pl.BlockSpec((None, g, 4, 4, None, None, SUBLANES, width),
                            lambda b, r, i: (b, i, 0, 0, r // 4, r % 4, 0, 0))
    if dilation == 4:
        return pl.BlockSpec((None, None, g, 4, 4, None, SUBLANES, width),
                            lambda b, r, i: (b, i // per, i % per, 0, 0, r, 0, 0))
    if dilation == 1:
        return pl.BlockSpec((None, None, None, g, 4, 4, SUBLANES, width),
                            lambda b, r, i: (b, i // (4 * per), (i // per) % 4, i % per, 0, 0, 0, 0))
    raise NotImplementedError(dilation)


def _attn_stage(q, k, v, dilation):
    bsz, seq, d = q.shape
    nb = seq // (dilation * QB)
    view = lambda a: a.reshape(bsz, seq // 2048, 4, 4, 4, 4, SUBLANES, a.shape[-1])
    bias = _alibi_bias(dilation)
    bias_spec = pl.BlockSpec(bias.shape, lambda b, r, i: (0, 0, 0, 0),
                             pipeline_mode=pl.Buffered(1))
    box = _box_spec(dilation, d)
    o, lse = pl.pallas_call(
        _attn_kernel,
        grid=(bsz, dilation, nb // ATTN_BLOCKS),
        in_specs=[bias_spec, box, box, box],
        out_specs=[box, _box_spec(dilation, LANES)],
        out_shape=[jax.ShapeDtypeStruct(view(q).shape, F32),
                   jax.ShapeDtypeStruct(view(q).shape[:-1] + (LANES,), F32)],
        scratch_shapes=[pltpu.VMEM((ATTN_BLOCKS, QB, d), BF16)] * 2,
        compiler_params=pltpu.CompilerParams(
            dimension_semantics=("arbitrary", "arbitrary", "arbitrary"),
            vmem_limit_bytes=VMEM_LIMIT_BYTES),
        name=f"attn_stage_d{dilation}",
    )(bias, view(q), view(k), view(v))
    return o.reshape(bsz * seq, d), lse.reshape(bsz * seq, LANES)


def _out_kernel(*refs, n_patterns):
    x_ref, ac_ref, za_ref, gc_ref, ga_ref = refs[:5]
    o_refs = refs[5:5 + n_patterns]
    lse_refs = refs[5 + n_patterns:5 + 2 * n_patterns]
    spread_ref, perm_ref, woc_ref, woa_ref, wo_ref, fg_ref, y_ref = refs[5 + 2 * n_patterns:]

    lses = [r[...] for r in lse_refs]
    top = functools.reduce(jnp.maximum, lses)
    ws = [jnp.exp(l - top) for l in lses]
    total = functools.reduce(jnp.add, ws)
    mixed_o = None
    for w, o_ref in zip(ws, o_refs):
        alpha = w / total
        hi = alpha.astype(BF16)
        lo = (alpha - hi.astype(F32)).astype(BF16)
        wide = jnp.dot(jnp.concatenate([hi, lo], axis=1), spread_ref[...],
                       preferred_element_type=F32)
        term = wide * o_ref[...]
        mixed_o = term if mixed_o is None else mixed_o + term
    gated = (za_ref[...].astype(F32) * mixed_o).astype(BF16)
    y_attn = jnp.dot(gated, woa_ref[...], preferred_element_type=F32)
    y_conv = jnp.dot(ac_ref[...], woc_ref[...], preferred_element_type=F32)
    merged = gc_ref[...].astype(F32) * y_conv + ga_ref[...].astype(F32) * y_attn
    merged = _permute_rows(perm_ref[...], merged.astype(BF16))
    mixed = jnp.dot(merged, wo_ref[...], preferred_element_type=F32)
    y_ref[...] = _rms_norm(x_ref[...] + mixed, fg_ref[...])


def _head_spread(d):
    s = np.zeros((2 * LANES, d), np.float32)
    for h in range(N_HEADS):
        s[h, h * HEAD_DIM:(h + 1) * HEAD_DIM] = 1.0
        s[LANES + h, h * HEAD_DIM:(h + 1) * HEAD_DIM] = 1.0
    return jnp.asarray(s, BF16)


def _out_stage(x, ac, za, gc, ga, os_, lses, w_out_conv, w_out_attn, w_o, final_g):
    n, d = x.shape
    rows = OUT_ROWS
    n_patterns = len(os_)
    tile = pl.BlockSpec((rows, d), lambda i: (i, 0))
    stat = pl.BlockSpec((rows, LANES), lambda i: (i, 0))
    const = lambda shape: pl.BlockSpec(shape, lambda i: (0,) * len(shape),
                                       pipeline_mode=pl.Buffered(1))
    return pl.pallas_call(
        functools.partial(_out_kernel, n_patterns=n_patterns),
        grid=(n // rows,),
        in_specs=([tile] * (5 + n_patterns) + [stat] * n_patterns
                  + [const((2 * LANES, d)), const((rows, rows))]
                  + [const((d, d))] * 3 + [const((1, d))]),
        out_specs=tile,
        out_shape=jax.ShapeDtypeStruct((n, d), F32),
        compiler_params=pltpu.CompilerParams(
            dimension_semantics=("arbitrary",),
            vmem_limit_bytes=VMEM_LIMIT_BYTES),
        name="out_stage",
    )(x, ac, za, gc, ga, *os_, *lses, _head_spread(d), _to_storage_order(rows).T,
      w_out_conv, w_out_attn, w_o, final_g)


def _layer(h, norm_g, w_in, b_merge, conv_w, w_out_conv, w_out_attn, w_o, out_g):
    bsz, seq, d = h.shape
    q, k, v, za, ac, gc, ga = _proj_stage(
        h, norm_g[None, :], w_in.astype(BF16), b_merge[None, :], conv_w)
    os_, lses = zip(*[_attn_stage(q, k, v, dilation) for _, dilation in ATTN_PATTERNS])
    flat = lambda a: a.reshape(bsz * seq, d)
    y = _out_stage(flat(h), flat(ac), flat(za), flat(gc), flat(ga), os_, lses,
                   w_out_conv.astype(BF16), w_out_attn.astype(BF16), w_o.astype(BF16),
                   out_g[None, :])
    return y.reshape(bsz, seq, d)


def kernel(x, norm_g, w_in, b_merge, conv_w, w_out_conv, w_out_attn, w_o, final_g):
    depth = norm_g.shape[0]
    assert depth == 1, "the fused output stage applies the final norm after a single layer"
    assert x.shape[-1] == N_HEADS * HEAD_DIM and x.shape[1] % (2048 * ATTN_BLOCKS) == 0
    assert ATTN_PATTERNS == ((128, 1), (512, 4), (2048, 16)), "box specs are per pattern"
    return _layer(x, norm_g[0], w_in[0], b_merge[0], conv_w[0],
                  w_out_conv[0], w_out_attn[0], w_o[0], final_g)
```

```python
import functools

import jax
import jax.numpy as jnp
import numpy as np
from jax import lax
from jax.experimental import pallas as pl
from jax.experimental.pallas import tpu as pltpu

N_HEADS = 16
HEAD_DIM = 64
CONV_WIDTH = 3
ATTN_PATTERNS = ((128, 1), (512, 4), (2048, 16))
QB = 128
EPS = 1e-6
LOG2_E = 1.4426950408889634

LANES = 128
SUBLANES = 8
HEADS_PER_LANE_TILE = LANES // HEAD_DIM
GROUPS = QB // SUBLANES
ATTN_BLOCKS = 2
PROJ_ROWS = 256
OUT_ROWS = 256
VMEM_LIMIT_BYTES = 56 * 1024 * 1024

BF16 = jnp.bfloat16
F32 = jnp.float32


def _rms_norm(x, g):
    return x * lax.rsqrt(jnp.mean(x * x, axis=-1, keepdims=True) + EPS) * g


def _to_storage_order(rows):
    s = np.arange(rows)
    natural = (s // QB) * QB + (s % SUBLANES) * GROUPS + (s % QB) // SUBLANES
    p = np.zeros((rows, rows), np.float32)
    p[s, natural] = 1.0
    return jnp.asarray(p, BF16)


def _permute_rows(p, a):
    return jnp.dot(p, a, preferred_element_type=F32).astype(BF16)


def _previous_token(a, carry_group):
    rows, d = a.shape
    first_row = lax.broadcasted_iota(jnp.int32, (SUBLANES, d), 0) == 0
    pieces = []
    prev_last = carry_group
    for blk in range(rows // QB):
        base = blk * QB
        last = a[base + QB - SUBLANES:base + QB, :]
        pieces.append(jnp.where(first_row, pltpu.roll(prev_last, 1, 0), pltpu.roll(last, 1, 0)))
        pieces.append(a[base:base + QB - SUBLANES, :])
        prev_last = last
    return jnp.concatenate(pieces, axis=0)


def _proj_kernel(x_ref, perm_ref, g_ref, w_ref, b_ref, cw_ref,
                 q_ref, k_ref, v_ref, za_ref, ac_ref, gc_ref, ga_ref, carry):
    rows, d = x_ref.shape
    u = _permute_rows(perm_ref[...], _rms_norm(x_ref[...], g_ref[...]).astype(BF16))

    def proj(c):
        return jnp.dot(u, w_ref[:, c * d:(c + 1) * d], preferred_element_type=F32)

    @pl.when(pl.program_id(1) == 0)
    def _():
        carry[...] = jnp.zeros(carry.shape, F32)

    a = proj(2) * proj(0)
    back1 = _previous_token(a, carry[SUBLANES:2 * SUBLANES, :])
    back2 = _previous_token(back1, carry[0:SUBLANES, :])
    carry[...] = a[rows - 2 * SUBLANES:rows, :]
    c = cw_ref[2:3, :] * a + cw_ref[1:2, :] * back1 + cw_ref[0:1, :] * back2
    ac_ref[...] = (jax.nn.silu(proj(3)) * proj(1) * c).astype(BF16)

    q_ref[...] = proj(4) * (HEAD_DIM ** -0.5 * LOG2_E)
    k_ref[...] = proj(5)
    v_ref[...] = proj(6)
    za_ref[...] = jax.nn.silu(proj(7)).astype(BF16)
    gc_ref[...] = jax.nn.sigmoid(proj(8) + b_ref[:, 0:d]).astype(BF16)
    ga_ref[...] = jax.nn.sigmoid(proj(9) + b_ref[:, d:2 * d]).astype(BF16)


def _proj_stage(x, norm_g, w_in, b_merge, conv_w):
    bsz, seq, d = x.shape
    rows = PROJ_ROWS
    tile = pl.BlockSpec((None, rows, d), lambda b, i: (b, i, 0))
    const = lambda shape: pl.BlockSpec(shape, lambda b, i: (0,) * len(shape),
                                       pipeline_mode=pl.Buffered(1))
    wide = jax.ShapeDtypeStruct((bsz, seq, d), F32)
    narrow = jax.ShapeDtypeStruct((bsz, seq, d), BF16)
    return pl.pallas_call(
        _proj_kernel,
        grid=(bsz, seq // rows),
        in_specs=[tile, const((rows, rows)), const((1, d)), const(w_in.shape),
                  const((1, 2 * d)), const((CONV_WIDTH, d))],
        out_specs=[tile] * 7,
        out_shape=[wide] * 3 + [narrow] * 4,
        scratch_shapes=[pltpu.VMEM((2 * SUBLANES, d), F32)],
        compiler_params=pltpu.CompilerParams(
            dimension_semantics=("arbitrary", "arbitrary"),
            vmem_limit_bytes=VMEM_LIMIT_BYTES),
        name="proj_stage",
    )(x, _to_storage_order(rows), norm_g, w_in, b_merge, conv_w)


def _attn_kernel(bias_ref, q_ref, k_ref, v_ref, o_ref, m_ref, l_ref, kbuf, vbuf):
    n_blocks, d = q_ref.shape[0], q_ref.shape[-1]
    step = pl.program_id(2)
    first = (step == 0).astype(jnp.int32)

    @pl.when(step == 0)
    def _():
        kbuf[0] = jnp.zeros((QB, d), BF16)
        vbuf[0] = jnp.zeros((QB, d), BF16)

    lane = lax.broadcasted_iota(jnp.int32, (QB, LANES), 1)
    head_masks = [(lane // HEAD_DIM) == e for e in range(HEADS_PER_LANE_TILE)]

    for g in range(n_blocks):
        keep = (g + 1) % n_blocks
        m_tile = jnp.zeros((QB, LANES), F32)
        l_tile = jnp.ones((QB, LANES), F32)
        for t in range(N_HEADS // HEADS_PER_LANE_TILE):
            cols = slice(t * LANES, (t + 1) * LANES)
            block = lambda ref: ref[g, :, :, :, cols].reshape(QB, LANES).astype(BF16)
            q, k_cur, v_cur = block(q_ref), block(k_ref), block(v_ref)
            k = jnp.concatenate([kbuf[g, :, cols], k_cur], axis=0)
            v = jnp.concatenate([vbuf[g, :, cols], v_cur], axis=0)
            kbuf[keep, :, cols] = k_cur
            vbuf[keep, :, cols] = v_cur
            q_heads = jnp.concatenate(
                [jnp.where(m, q, jnp.zeros_like(q)) for m in head_masks], axis=0)
            s_all = lax.dot_general(q_heads, k, (((1,), (1,)), ((), ())),
                                    preferred_element_type=F32)
            ps = []
            for e in range(HEADS_PER_LANE_TILE):
                h = t * HEADS_PER_LANE_TILE + e
                s = s_all[e * QB:(e + 1) * QB, :] + bias_ref[first if g == 0 else 0, h]
                m = jnp.max(s, axis=-1, keepdims=True)
                p = jnp.exp2(s - m)
                m_tile = jnp.where(lane == h, m, m_tile)
                l_tile = jnp.where(lane == h, jnp.sum(p, axis=-1, keepdims=True), l_tile)
                ps.append(p.astype(BF16))
            acc_all = jnp.dot(jnp.concatenate(ps, axis=0), v, preferred_element_type=F32)
            o_tile = acc_all[0:QB, :]
            for e in range(1, HEADS_PER_LANE_TILE):
                o_tile = jnp.where(head_masks[e], acc_all[e * QB:(e + 1) * QB, :], o_tile)
            o_ref[g, :, :, :, cols] = o_tile.reshape(o_ref.shape[1:-1] + (LANES,))
        m_ref[g] = m_tile.reshape(m_ref.shape[1:])
        l_ref[g] = l_tile.reshape(l_ref.shape[1:])


def _block_positions(dilation):
    row = np.arange(QB)
    if dilation == 16:
        return row
    if dilation == 4:
        return (row // 32) * 32 + (row % 8) * 4 + (row // 8) % 4
    if dilation == 1:
        return (row % 8) * 16 + row // 8
    raise NotImplementedError(dilation)


def _alibi_bias(dilation):
    pos = _block_positions(dilation)
    q_loc = pos[:, None] + QB
    k_loc = np.concatenate([pos, pos + QB])[None, :]
    delta = q_loc - k_loc
    valid = (delta >= 0) & (delta <= QB)
    has_prev = np.stack([np.ones((2 * QB,), bool), np.arange(2 * QB) >= QB])
    ok = jnp.asarray(valid[None, None] & has_prev[:, None, None, :])
    slopes = jnp.exp2(-8.0 * jnp.arange(1, N_HEADS + 1, dtype=F32) / N_HEADS)
    bias = -slopes[:, None, None] * jnp.asarray(dilation * delta, F32)[None]
    return jnp.where(ok, bias[None] * LOG2_E, -jnp.inf)


def _box_spec(dilation, width):
    g = ATTN_BLOCKS
    per = 4 // g
    if dilation == 16:
        return pl.BlockSpec((None, g, 4, 4, None, None, SUBLANES, width),
                            lambda b, r, i: (b, i, 0, 0, r // 4, r % 4, 0, 0))
    if dilation == 4:
        return pl.BlockSpec((None, None, g, 4, 4, None, SUBLANES, width),
                            lambda b, r, i: (b, i // per, i % per, 0, 0, r, 0, 0))
    if dilation == 1:
        return pl.BlockSpec((None, None, None, g, 4, 4, SUBLANES, width),
                            lambda b, r, i: (b, i // (4 * per), (i // per) % 4, i % per, 0, 0, 0, 0))
    raise NotImplementedError(dilation)


def _attn_stage(q, k, v, dilation):
    bsz, seq, d = q.shape
    nb = seq // (dilation * QB)
    view = lambda a: a.reshape(bsz, seq // 2048, 4, 4, 4, 4, SUBLANES, a.shape[-1])
    bias = _alibi_bias(dilation)
    bias_spec = pl.BlockSpec(bias.shape, lambda b, r, i: (0, 0, 0, 0),
                             pipeline_mode=pl.Buffered(1))
    box = _box_spec(dilation, d)
    stat_box = _box_spec(dilation, LANES)
    stat_shape = jax.ShapeDtypeStruct(view(q).shape[:-1] + (LANES,), F32)
    o, m, l = pl.pallas_call(
        _attn_kernel,
        grid=(bsz, dilation, nb // ATTN_BLOCKS),
        in_specs=[bias_spec, box, box, box],
        out_specs=[box, stat_box, stat_box],
        out_shape=[jax.ShapeDtypeStruct(view(q).shape, F32), stat_shape, stat_shape],
        scratch_shapes=[pltpu.VMEM((ATTN_BLOCKS, QB, d), BF16)] * 2,
        compiler_params=pltpu.CompilerParams(
            dimension_semantics=("arbitrary", "arbitrary", "arbitrary"),
            vmem_limit_bytes=VMEM_LIMIT_BYTES),
        name=f"attn_stage_d{dilation}",
    )(bias, view(q), view(k), view(v))
    return o.reshape(bsz * seq, d), m.reshape(bsz * seq, LANES), l.reshape(bsz * seq, LANES)


def _out_kernel(*refs, n_patterns):
    x_ref, ac_ref, za_ref, gc_ref, ga_ref = refs[:5]
    o_refs = refs[5:5 + n_patterns]
    m_refs = refs[5 + n_patterns:5 + 2 * n_patterns]
    l_refs = refs[5 + 2 * n_patterns:5 + 3 * n_patterns]
    spread_ref, perm_ref, woc_ref, woa_ref, wo_ref, fg_ref, y_ref = refs[5 + 3 * n_patterns:]

    ms = [r[...] for r in m_refs]
    top = functools.reduce(jnp.maximum, ms)
    ws = [jnp.exp2(m - top) for m in ms]
    total = functools.reduce(jnp.add, [w * r[...] for w, r in zip(ws, l_refs)])
    mixed_o = None
    for w, o_ref in zip(ws, o_refs):
        alpha = w / total
        hi = alpha.astype(BF16)
        lo = (alpha - hi.astype(F32)).astype(BF16)
        wide = jnp.dot(jnp.concatenate([hi, lo], axis=1), spread_ref[...],
                       preferred_element_type=F32)
        term = wide * o_ref[...]
        mixed_o = term if mixed_o is None else mixed_o + term
    gated = (za_ref[...].astype(F32) * mixed_o).astype(BF16)
    y_attn = jnp.dot(gated, woa_ref[...], preferred_element_type=F32)
    y_conv = jnp.dot(ac_ref[...], woc_ref[...], preferred_element_type=F32)
    merged = gc_ref[...].astype(F32) * y_conv + ga_ref[...].astype(F32) * y_attn
    merged = _permute_rows(perm_ref[...], merged.astype(BF16))
    mixed = jnp.dot(merged, wo_ref[...], preferred_element_type=F32)
    y_ref[...] = _rms_norm(x_ref[...] + mixed, fg_ref[...])


def _head_spread(d):
    s = np.zeros((2 * LANES, d), np.float32)
    for h in range(N_HEADS):
        s[h, h * HEAD_DIM:(h + 1) * HEAD_DIM] = 1.0
        s[LANES + h, h * HEAD_DIM:(h + 1) * HEAD_DIM] = 1.0
    return jnp.asarray(s, BF16)


def _out_stage(x, ac, za, gc, ga, os_, ms, ls, w_out_conv, w_out_attn, w_o, final_g):
    n, d = x.shape
    rows = OUT_ROWS
    n_patterns = len(os_)
    tile = pl.BlockSpec((rows, d), lambda i: (i, 0))
    stat = pl.BlockSpec((rows, LANES), lambda i: (i, 0))
    const = lambda shape: pl.BlockSpec(shape, lambda i: (0,) * len(shape),
                                       pipeline_mode=pl.Buffered(1))
    return pl.pallas_call(
        functools.partial(_out_kernel, n_patterns=n_patterns),
        grid=(n // rows,),
        in_specs=([tile] * (5 + n_patterns) + [stat] * (2 * n_patterns)
                  + [const((2 * LANES, d)), const((rows, rows))]
                  + [const((d, d))] * 3 + [const((1, d))]),
        out_specs=tile,
        out_shape=jax.ShapeDtypeStruct((n, d), F32),
        compiler_params=pltpu.CompilerParams(
            dimension_semantics=("arbitrary",),
            vmem_limit_bytes=VMEM_LIMIT_BYTES),
        name="out_stage",
    )(x, ac, za, gc, ga, *os_, *ms, *ls, _head_spread(d), _to_storage_order(rows).T,
      w_out_conv, w_out_attn, w_o, final_g)


def _layer(h, norm_g, w_in, b_merge, conv_w, w_out_conv, w_out_attn, w_o, out_g):
    bsz, seq, d = h.shape
    q, k, v, za, ac, gc, ga = _proj_stage(
        h, norm_g[None, :], w_in.astype(BF16), b_merge[None, :], conv_w)
    os_, ms, ls = zip(*[_attn_stage(q, k, v, dilation) for _, dilation in ATTN_PATTERNS])
    flat = lambda a: a.reshape(bsz * seq, d)
    y = _out_stage(flat(h), flat(ac), flat(za), flat(gc), flat(ga), os_, ms, ls,
                   w_out_conv.astype(BF16), w_out_attn.astype(BF16), w_o.astype(BF16),
                   out_g[None, :])
    return y.reshape(bsz, seq, d)


def kernel(x, norm_g, w_in, b_merge, conv_w, w_out_conv, w_out_attn, w_o, final_g):
    depth = norm_g.shape[0]
    assert depth == 1, "the fused output stage applies the final norm after a single layer"
    assert x.shape[-1] == N_HEADS * HEAD_DIM and x.shape[1] % (2048 * ATTN_BLOCKS) == 0
    assert ATTN_PATTERNS == ((128, 1), (512, 4), (2048, 16)), "box specs are per pattern"
    return _layer(x, norm_g[0], w_in[0], b_merge[0], conv_w[0],
                  w_out_conv[0], w_out_attn[0], w_o[0], final_g)
```

```python
import functools

import jax
import jax.numpy as jnp
import numpy as np
from jax import lax
from jax.experimental import pallas as pl
from jax.experimental.pallas import tpu as pltpu

N_HEADS = 16
HEAD_DIM = 64
CONV_WIDTH = 3
ATTN_PATTERNS = ((128, 1), (512, 4), (2048, 16))
QB = 128
EPS = 1e-6
LOG2_E = 1.4426950408889634

LANES = 128
SUBLANES = 8
HEADS_PER_LANE_TILE = LANES // HEAD_DIM
GROUPS = QB // SUBLANES
ATTN_BLOCKS = 4
PROJ_ROWS = 256
OUT_ROWS = 256
VMEM_LIMIT_BYTES = 56 * 1024 * 1024

BF16 = jnp.bfloat16
F32 = jnp.float32


def _rms_norm(x, g):
    return x * lax.rsqrt(jnp.mean(x * x, axis=-1, keepdims=True) + EPS) * g


def _to_storage_order(rows):
    s = np.arange(rows)
    natural = (s // QB) * QB + (s % SUBLANES) * GROUPS + (s % QB) // SUBLANES
    p = np.zeros((rows, rows), np.float32)
    p[s, natural] = 1.0
    return jnp.asarray(p, BF16)


def _permute_rows(p, a):
    return jnp.dot(p, a, preferred_element_type=F32).astype(BF16)


def _previous_token(a, carry_group):
    rows, d = a.shape
    first_row = lax.broadcasted_iota(jnp.int32, (SUBLANES, d), 0) == 0
    pieces = []
    prev_last = carry_group
    for blk in range(rows // QB):
        base = blk * QB
        last = a[base + QB - SUBLANES:base + QB, :]
        pieces.append(jnp.where(first_row, pltpu.roll(prev_last, 1, 0), pltpu.roll(last, 1, 0)))
        pieces.append(a[base:base + QB - SUBLANES, :])
        prev_last = last
    return jnp.concatenate(pieces, axis=0)


def _proj_kernel(x_ref, perm_ref, g_ref, w_ref, b_ref, cw_ref,
                 q_ref, k_ref, v_ref, za_ref, ac_ref, gc_ref, ga_ref, carry):
    rows, d = x_ref.shape
    u = _permute_rows(perm_ref[...], _rms_norm(x_ref[...], g_ref[...]).astype(BF16))

    def proj(c):
        return jnp.dot(u, w_ref[:, c * d:(c + 1) * d], preferred_element_type=F32)

    @pl.when(pl.program_id(1) == 0)
    def _():
        carry[...] = jnp.zeros(carry.shape, F32)

    a = proj(2) * proj(0)
    back1 = _previous_token(a, carry[SUBLANES:2 * SUBLANES, :])
    back2 = _previous_token(back1, carry[0:SUBLANES, :])
    carry[...] = a[rows - 2 * SUBLANES:rows, :]
    c = cw_ref[2:3, :] * a + cw_ref[1:2, :] * back1 + cw_ref[0:1, :] * back2
    ac_ref[...] = (jax.nn.silu(proj(3)) * proj(1) * c).astype(BF16)

    q_ref[...] = proj(4) * (HEAD_DIM ** -0.5 * LOG2_E)
    k_ref[...] = proj(5)
    v_ref[...] = proj(6)
    za_ref[...] = jax.nn.silu(proj(7)).astype(BF16)
    gc_ref[...] = jax.nn.sigmoid(proj(8) + b_ref[:, 0:d]).astype(BF16)
    ga_ref[...] = jax.nn.sigmoid(proj(9) + b_ref[:, d:2 * d]).astype(BF16)


def _proj_stage(x, norm_g, w_in, b_merge, conv_w):
    bsz, seq, d = x.shape
    rows = PROJ_ROWS
    tile = pl.BlockSpec((None, rows, d), lambda b, i: (b, i, 0))
    const = lambda shape: pl.BlockSpec(shape, lambda b, i: (0,) * len(shape),
                                       pipeline_mode=pl.Buffered(1))
    wide = jax.ShapeDtypeStruct((bsz, seq, d), F32)
    narrow = jax.ShapeDtypeStruct((bsz, seq, d), BF16)
    return pl.pallas_call(
        _proj_kernel,
        grid=(bsz, seq // rows),
        in_specs=[tile, const((rows, rows)), const((1, d)), const(w_in.shape),
                  const((1, 2 * d)), const((CONV_WIDTH, d))],
        out_specs=[tile] * 7,
        out_shape=[wide] * 3 + [narrow] * 4,
        scratch_shapes=[pltpu.VMEM((2 * SUBLANES, d), F32)],
        compiler_params=pltpu.CompilerParams(
            dimension_semantics=("arbitrary", "arbitrary"),
            vmem_limit_bytes=VMEM_LIMIT_BYTES),
        name="proj_stage",
    )(x, _to_storage_order(rows), norm_g, w_in, b_merge, conv_w)


def _attn_kernel(bias_ref, q_ref, k_ref, v_ref, o_ref, m_ref, l_ref, kbuf, vbuf):
    n_blocks, d = q_ref.shape[0], q_ref.shape[-1]
    step = pl.program_id(2)
    first = (step == 0).astype(jnp.int32)

    @pl.when(step == 0)
    def _():
        kbuf[0] = jnp.zeros((QB, d), BF16)
        vbuf[0] = jnp.zeros((QB, d), BF16)

    lane = lax.broadcasted_iota(jnp.int32, (QB, LANES), 1)
    head_masks = [(lane // HEAD_DIM) == e for e in range(HEADS_PER_LANE_TILE)]

    for g in range(n_blocks):
        keep = (g + 1) % n_blocks
        m_tile = jnp.zeros((QB, LANES), F32)
        l_tile = jnp.ones((QB, LANES), F32)
        for t in range(N_HEADS // HEADS_PER_LANE_TILE):
            cols = slice(t * LANES, (t + 1) * LANES)
            block = lambda ref: ref[g, :, :, :, cols].reshape(QB, LANES).astype(BF16)
            q, k_cur, v_cur = block(q_ref), block(k_ref), block(v_ref)
            k = jnp.concatenate([kbuf[g, :, cols], k_cur], axis=0)
            v = jnp.concatenate([vbuf[g, :, cols], v_cur], axis=0)
            kbuf[keep, :, cols] = k_cur
            vbuf[keep, :, cols] = v_cur
            q_heads = jnp.concatenate(
                [jnp.where(m, q, jnp.zeros_like(q)) for m in head_masks], axis=0)
            s_all = lax.dot_general(q_heads, k, (((1,), (1,)), ((), ())),
                                    preferred_element_type=F32)
            ps = []
            for e in range(HEADS_PER_LANE_TILE):
                h = t * HEADS_PER_LANE_TILE + e
                s = s_all[e * QB:(e + 1) * QB, :] + bias_ref[first if g == 0 else 0, h]
                m = jnp.max(s, axis=-1, keepdims=True)
                p = jnp.exp2(s - m)
                m_tile = jnp.where(lane == h, m, m_tile)
                l_tile = jnp.where(lane == h, jnp.sum(p, axis=-1, keepdims=True), l_tile)
                ps.append(p.astype(BF16))
            acc_all = jnp.dot(jnp.concatenate(ps, axis=0), v, preferred_element_type=F32)
            o_tile = acc_all[0:QB, :]
            for e in range(1, HEADS_PER_LANE_TILE):
                o_tile = jnp.where(head_masks[e], acc_all[e * QB:(e + 1) * QB, :], o_tile)
            o_ref[g, :, :, :, cols] = o_tile.reshape(o_ref.shape[1:-1] + (LANES,))
        m_ref[g] = m_tile.reshape(m_ref.shape[1:])
        l_ref[g] = l_tile.reshape(l_ref.shape[1:])


def _block_positions(dilation):
    row = np.arange(QB)
    if dilation == 16:
        return row
    if dilation == 4:
        return (row // 32) * 32 + (row % 8) * 4 + (row // 8) % 4
    if dilation == 1:
        return (row % 8) * 16 + row // 8
    raise NotImplementedError(dilation)


def _alibi_bias(dilation):
    pos = _block_positions(dilation)
    q_loc = pos[:, None] + QB
    k_loc = np.concatenate([pos, pos + QB])[None, :]
    delta = q_loc - k_loc
    valid = (delta >= 0) & (delta <= QB)
    has_prev = np.stack([np.ones((2 * QB,), bool), np.arange(2 * QB) >= QB])
    ok = jnp.asarray(valid[None, None] & has_prev[:, None, None, :])
    slopes = jnp.exp2(-8.0 * jnp.arange(1, N_HEADS + 1, dtype=F32) / N_HEADS)
    bias = -slopes[:, None, None] * jnp.asarray(dilation * delta, F32)[None]
    return jnp.where(ok, bias[None] * LOG2_E, -jnp.inf)


def _box_spec(dilation, width, g):
    per = 4 // g
    if dilation == 16:
        return pl.BlockSpec((None, g, 4, 4, None, None, SUBLANES, width),
                            lambda b, r, i: (b, i, 0, 0, r // 4, r % 4, 0, 0))
    if dilation == 4:
        return pl.BlockSpec((None, None, g, 4, 4, None, SUBLANES, width),
                            lambda b, r, i: (b, i // per, i % per, 0, 0, r, 0, 0))
    if dilation == 1:
        return pl.BlockSpec((None, None, None, g, 4, 4, SUBLANES, width),
                            lambda b, r, i: (b, i // (4 * per), (i // per) % 4, i % per, 0, 0, 0, 0))
    raise NotImplementedError(dilation)


def _attn_stage(q, k, v, dilation):
    bsz, seq, d = q.shape
    nb = seq // (dilation * QB)
    view = lambda a: a.reshape(bsz, seq // 2048, 4, 4, 4, 4, SUBLANES, a.shape[-1])
    bias = _alibi_bias(dilation)
    bias_spec = pl.BlockSpec(bias.shape, lambda b, r, i: (0, 0, 0, 0),
                             pipeline_mode=pl.Buffered(1))
    g = min(ATTN_BLOCKS, nb)
    box = _box_spec(dilation, d, g)
    stat_box = _box_spec(dilation, LANES, g)
    stat_shape = jax.ShapeDtypeStruct(view(q).shape[:-1] + (LANES,), F32)
    o, m, l = pl.pallas_call(
        _attn_kernel,
        grid=(bsz, dilation, nb // g),
        in_specs=[bias_spec, box, box, box],
        out_specs=[box, stat_box, stat_box],
        out_shape=[jax.ShapeDtypeStruct(view(q).shape, F32), stat_shape, stat_shape],
        scratch_shapes=[pltpu.VMEM((g, QB, d), BF16)] * 2,
        compiler_params=pltpu.CompilerParams(
            dimension_semantics=("arbitrary", "arbitrary", "arbitrary"),
            vmem_limit_bytes=VMEM_LIMIT_BYTES),
        name=f"attn_stage_d{dilation}",
    )(bias, view(q), view(k), view(v))
    return o.reshape(bsz * seq, d), m.reshape(bsz * seq, LANES), l.reshape(bsz * seq, LANES)


def _out_kernel(*refs, n_patterns):
    x_ref, ac_ref, za_ref, gc_ref, ga_ref = refs[:5]
    o_refs = refs[5:5 + n_patterns]
    m_refs = refs[5 + n_patterns:5 + 2 * n_patterns]
    l_refs = refs[5 + 2 * n_patterns:5 + 3 * n_patterns]
    spread_ref, perm_ref, woc_ref, woa_ref, wo_ref, fg_ref, y_ref = refs[5 + 3 * n_patterns:]

    ms = [r[...] for r in m_refs]
    top = functools.reduce(jnp.maximum, ms)
    ws = [jnp.exp2(m - top) for m in ms]
    total = functools.reduce(jnp.add, [w * r[...] for w, r in zip(ws, l_refs)])
    mixed_o = None
    for w, o_ref in zip(ws, o_refs):
        alpha = w / total
        hi = alpha.astype(BF16)
        lo = (alpha - hi.astype(F32)).astype(BF16)
        wide = jnp.dot(jnp.concatenate([hi, lo], axis=1), spread_ref[...],
                       preferred_element_type=F32)
        term = wide * o_ref[...]
        mixed_o = term if mixed_o is None else mixed_o + term
    gated = (za_ref[...].astype(F32) * mixed_o).astype(BF16)
    y_attn = jnp.dot(gated, woa_ref[...], preferred_element_type=F32)
    y_conv = jnp.dot(ac_ref[...], woc_ref[...], preferred_element_type=F32)
    merged = gc_ref[...].astype(F32) * y_conv + ga_ref[...].astype(F32) * y_attn
    merged = _permute_rows(perm_ref[...], merged.astype(BF16))
    mixed = jnp.dot(merged, wo_ref[...], preferred_element_type=F32)
    y_ref[...] = _rms_norm(x_ref[...] + mixed, fg_ref[...])


def _head_spread(d):
    s = np.zeros((2 * LANES, d), np.float32)
    for h in range(N_HEADS):
        s[h, h * HEAD_DIM:(h + 1) * HEAD_DIM] = 1.0
        s[LANES + h, h * HEAD_DIM:(h + 1) * HEAD_DIM] = 1.0
    return jnp.asarray(s, BF16)


def _out_stage(x, ac, za, gc, ga, os_, ms, ls, w_out_conv, w_out_attn, w_o, final_g):
    n, d = x.shape
    rows = OUT_ROWS
    n_patterns = len(os_)
    tile = pl.BlockSpec((rows, d), lambda i: (i, 0))
    stat = pl.BlockSpec((rows, LANES), lambda i: (i, 0))
    const = lambda shape: pl.BlockSpec(shape, lambda i: (0,) * len(shape),
                                       pipeline_mode=pl.Buffered(1))
    return pl.pallas_call(
        functools.partial(_out_kernel, n_patterns=n_patterns),
        grid=(n // rows,),
        in_specs=([tile] * (5 + n_patterns) + [stat] * (2 * n_patterns)
                  + [const((2 * LANES, d)), const((rows, rows))]
                  + [const((d, d))] * 3 + [const((1, d))]),
        out_specs=tile,
        out_shape=jax.ShapeDtypeStruct((n, d), F32),
        compiler_params=pltpu.CompilerParams(
            dimension_semantics=("arbitrary",),
            vmem_limit_bytes=VMEM_LIMIT_BYTES),
        name="out_stage",
    )(x, ac, za, gc, ga, *os_, *ms, *ls, _head_spread(d), _to_storage_order(rows).T,
      w_out_conv, w_out_attn, w_o, final_g)


def _layer(h, norm_g, w_in, b_merge, conv_w, w_out_conv, w_out_attn, w_o, out_g):
    bsz, seq, d = h.shape
    q, k, v, za, ac, gc, ga = _proj_stage(
        h, norm_g[None, :], w_in.astype(BF16), b_merge[None, :], conv_w)
    os_, ms, ls = zip(*[_attn_stage(q, k, v, dilation) for _, dilation in ATTN_PATTERNS])
    flat = lambda a: a.reshape(bsz * seq, d)
    y = _out_stage(flat(h), flat(ac), flat(za), flat(gc), flat(ga), os_, ms, ls,
                   w_out_conv.astype(BF16), w_out_attn.astype(BF16), w_o.astype(BF16),
                   out_g[None, :])
    return y.reshape(bsz, seq, d)


def kernel(x, norm_g, w_in, b_merge, conv_w, w_out_conv, w_out_attn, w_o, final_g):
    depth = norm_g.shape[0]
    assert depth == 1, "the fused output stage applies the final norm after a single layer"
    assert x.shape[-1] == N_HEADS * HEAD_DIM and x.shape[1] % 4096 == 0
    assert ATTN_PATTERNS == ((128, 1), (512, 4), (2048, 16)), "box specs are per pattern"
    return _layer(x, norm_g[0], w_in[0], b_merge[0], conv_w[0],
                  w_out_conv[0], w_out_attn[0], w_o[0], final_g)
```

```python
import functools

import jax
import jax.numpy as jnp
import numpy as np
from jax import lax
from jax.experimental import pallas as pl
from jax.experimental.pallas import tpu as pltpu

N_HEADS = 16
HEAD_DIM = 64
CONV_WIDTH = 3
ATTN_PATTERNS = ((128, 1), (512, 4), (2048, 16))
QB = 128
EPS = 1e-6
LOG2_E = 1.4426950408889634

LANES = 128
SUBLANES = 8
HEADS_PER_LANE_TILE = LANES // HEAD_DIM
GROUPS = QB // SUBLANES
ATTN_BLOCKS = 4
PROJ_ROWS = 256
OUT_ROWS = 256
VMEM_LIMIT_BYTES = 56 * 1024 * 1024

BF16 = jnp.bfloat16
F32 = jnp.float32


def _rms_norm(x, g):
    return x * lax.rsqrt(jnp.mean(x * x, axis=-1, keepdims=True) + EPS) * g


def _to_storage_order(rows):
    s = np.arange(rows)
    natural = (s // QB) * QB + (s % SUBLANES) * GROUPS + (s % QB) // SUBLANES
    p = np.zeros((rows, rows), np.float32)
    p[s, natural] = 1.0
    return jnp.asarray(p, BF16)


def _permute_rows(p, a):
    return jnp.dot(p, a, preferred_element_type=F32).astype(BF16)


def _previous_token(a, carry_group):
    rows, d = a.shape
    first_row = lax.broadcasted_iota(jnp.int32, (SUBLANES, d), 0) == 0
    pieces = []
    prev_last = carry_group
    for blk in range(rows // QB):
        base = blk * QB
        last = a[base + QB - SUBLANES:base + QB, :]
        pieces.append(jnp.where(first_row, pltpu.roll(prev_last, 1, 0), pltpu.roll(last, 1, 0)))
        pieces.append(a[base:base + QB - SUBLANES, :])
        prev_last = last
    return jnp.concatenate(pieces, axis=0)


def _proj_kernel(x_ref, perm_ref, g_ref, w_ref, b_ref, cw_ref,
                 q_ref, k_ref, v_ref, za_ref, ac_ref, gc_ref, ga_ref, carry):
    rows, d = x_ref.shape
    u = _permute_rows(perm_ref[...], _rms_norm(x_ref[...], g_ref[...]).astype(BF16))

    def proj(c):
        return jnp.dot(u, w_ref[:, c * d:(c + 1) * d], preferred_element_type=F32)

    @pl.when(pl.program_id(1) == 0)
    def _():
        carry[...] = jnp.zeros(carry.shape, F32)

    a = proj(2) * proj(0)
    back1 = _previous_token(a, carry[SUBLANES:2 * SUBLANES, :])
    back2 = _previous_token(back1, carry[0:SUBLANES, :])
    carry[...] = a[rows - 2 * SUBLANES:rows, :]
    c = cw_ref[2:3, :] * a + cw_ref[1:2, :] * back1 + cw_ref[0:1, :] * back2
    ac_ref[...] = (jax.nn.silu(proj(3)) * proj(1) * c).astype(BF16)

    q_ref[...] = proj(4) * (HEAD_DIM ** -0.5 * LOG2_E)
    k_ref[...] = proj(5)
    v_ref[...] = proj(6)
    za_ref[...] = jax.nn.silu(proj(7)).astype(BF16)
    gc_ref[...] = jax.nn.sigmoid(proj(8) + b_ref[:, 0:d]).astype(BF16)
    ga_ref[...] = jax.nn.sigmoid(proj(9) + b_ref[:, d:2 * d]).astype(BF16)


def _proj_stage(x, norm_g, w_in, b_merge, conv_w):
    bsz, seq, d = x.shape
    rows = PROJ_ROWS
    tile = pl.BlockSpec((None, rows, d), lambda b, i: (b, i, 0))
    const = lambda shape: pl.BlockSpec(shape, lambda b, i: (0,) * len(shape),
                                       pipeline_mode=pl.Buffered(1))
    wide = jax.ShapeDtypeStruct((bsz, seq, d), F32)
    narrow = jax.ShapeDtypeStruct((bsz, seq, d), BF16)
    return pl.pallas_call(
        _proj_kernel,
        grid=(bsz, seq // rows),
        in_specs=[tile, const((rows, rows)), const((1, d)), const(w_in.shape),
                  const((1, 2 * d)), const((CONV_WIDTH, d))],
        out_specs=[tile] * 7,
        out_shape=[wide] * 3 + [narrow] * 4,
        scratch_shapes=[pltpu.VMEM((2 * SUBLANES, d), F32)],
        compiler_params=pltpu.CompilerParams(
            dimension_semantics=("arbitrary", "arbitrary"),
            vmem_limit_bytes=VMEM_LIMIT_BYTES),
        name="proj_stage",
    )(x, _to_storage_order(rows), norm_g, w_in, b_merge, conv_w)


def _attn_kernel(bias_ref, q_ref, k_ref, v_ref, o_ref, m_ref, l_ref, kbuf, vbuf):
    n_blocks, d = q_ref.shape[0], q_ref.shape[-1]
    n_chains = q_ref.shape[3] if len(q_ref.shape) == 6 else 1
    step = pl.program_id(2)
    first = (step == 0).astype(jnp.int32)

    @pl.when(step == 0)
    def _():
        for c in range(n_chains):
            kbuf[c * n_blocks] = jnp.zeros((QB, d), BF16)
            vbuf[c * n_blocks] = jnp.zeros((QB, d), BF16)

    lane = lax.broadcasted_iota(jnp.int32, (QB, LANES), 1)
    head_masks = [(lane // HEAD_DIM) == e for e in range(HEADS_PER_LANE_TILE)]
    rows = (slice(None),) * 2

    for c, g in [(c, g) for c in range(n_chains) for g in range(n_blocks)]:
        chain = (c,) if len(q_ref.shape) == 6 else ()
        slot = c * n_blocks + g
        keep = c * n_blocks + (g + 1) % n_blocks
        m_tile = jnp.zeros((QB, LANES), F32)
        l_tile = jnp.ones((QB, LANES), F32)
        for t in range(N_HEADS // HEADS_PER_LANE_TILE):
            cols = slice(t * LANES, (t + 1) * LANES)
            at = (g,) + rows + chain + (slice(None), cols)
            block = lambda ref: ref[at].reshape(QB, LANES).astype(BF16)
            q, k_cur, v_cur = block(q_ref), block(k_ref), block(v_ref)
            k = jnp.concatenate([kbuf[slot, :, cols], k_cur], axis=0)
            v = jnp.concatenate([vbuf[slot, :, cols], v_cur], axis=0)
            kbuf[keep, :, cols] = k_cur
            vbuf[keep, :, cols] = v_cur
            q_heads = jnp.concatenate(
                [jnp.where(m, q, jnp.zeros_like(q)) for m in head_masks], axis=0)
            s_all = lax.dot_general(q_heads, k, (((1,), (1,)), ((), ())),
                                    preferred_element_type=F32)
            ps = []
            for e in range(HEADS_PER_LANE_TILE):
                h = t * HEADS_PER_LANE_TILE + e
                s = s_all[e * QB:(e + 1) * QB, :] + bias_ref[first if g == 0 else 0, h]
                m = jnp.max(s, axis=-1, keepdims=True)
                p = jnp.exp2(s - m)
                m_tile = jnp.where(lane == h, m, m_tile)
                l_tile = jnp.where(lane == h, jnp.sum(p, axis=-1, keepdims=True), l_tile)
                ps.append(p.astype(BF16))
            acc_all = jnp.dot(jnp.concatenate(ps, axis=0), v, preferred_element_type=F32)
            o_tile = acc_all[0:QB, :]
            for e in range(1, HEADS_PER_LANE_TILE):
                o_tile = jnp.where(head_masks[e], acc_all[e * QB:(e + 1) * QB, :], o_tile)
            o_ref[at] = o_tile.reshape(4, 4, SUBLANES, LANES)
        stat_at = (g,) + rows + chain
        m_ref[stat_at] = m_tile.reshape(4, 4, SUBLANES, LANES)
        l_ref[stat_at] = l_tile.reshape(4, 4, SUBLANES, LANES)


def _block_positions(dilation):
    row = np.arange(QB)
    if dilation == 16:
        return row
    if dilation == 4:
        return (row // 32) * 32 + (row % 8) * 4 + (row // 8) % 4
    if dilation == 1:
        return (row % 8) * 16 + row // 8
    raise NotImplementedError(dilation)


def _alibi_bias(dilation):
    pos = _block_positions(dilation)
    q_loc = pos[:, None] + QB
    k_loc = np.concatenate([pos, pos + QB])[None, :]
    delta = q_loc - k_loc
    valid = (delta >= 0) & (delta <= QB)
    has_prev = np.stack([np.ones((2 * QB,), bool), np.arange(2 * QB) >= QB])
    ok = jnp.asarray(valid[None, None] & has_prev[:, None, None, :])
    slopes = jnp.exp2(-8.0 * jnp.arange(1, N_HEADS + 1, dtype=F32) / N_HEADS)
    bias = -slopes[:, None, None] * jnp.asarray(dilation * delta, F32)[None]
    return jnp.where(ok, bias[None] * LOG2_E, -jnp.inf)


def _box_spec(dilation, width, g, chains):
    per = 4 // g
    if dilation == 16 and chains > 1:
        low = 4 // chains
        return pl.BlockSpec((None, g, 4, 4, None, chains, SUBLANES, width),
                            lambda b, r, i: (b, i, 0, 0, r // low, r % low, 0, 0))
    assert chains == 1
    if dilation == 16:
        return pl.BlockSpec((None, g, 4, 4, None, None, SUBLANES, width),
                            lambda b, r, i: (b, i, 0, 0, r // 4, r % 4, 0, 0))
    if dilation == 4:
        return pl.BlockSpec((None, None, g, 4, 4, None, SUBLANES, width),
                            lambda b, r, i: (b, i // per, i % per, 0, 0, r, 0, 0))
    if dilation == 1:
        return pl.BlockSpec((None, None, None, g, 4, 4, SUBLANES, width),
                            lambda b, r, i: (b, i // (4 * per), (i // per) % 4, i % per, 0, 0, 0, 0))
    raise NotImplementedError(dilation)


def _attn_stage(q, k, v, dilation):
    bsz, seq, d = q.shape
    nb = seq // (dilation * QB)
    view = lambda a: a.reshape(bsz, seq // 2048, 4, 4, 4, 4, SUBLANES, a.shape[-1])
    bias = _alibi_bias(dilation)
    bias_spec = pl.BlockSpec(bias.shape, lambda b, r, i: (0, 0, 0, 0),
                             pipeline_mode=pl.Buffered(1))
    g = min(ATTN_BLOCKS, nb)
    chains = min(ATTN_BLOCKS // g, 4) if dilation == 16 else 1
    box = _box_spec(dilation, d, g, chains)
    stat_box = _box_spec(dilation, LANES, g, chains)
    stat_shape = jax.ShapeDtypeStruct(view(q).shape[:-1] + (LANES,), F32)
    o, m, l = pl.pallas_call(
        _attn_kernel,
        grid=(bsz, dilation // chains, nb // g),
        in_specs=[bias_spec, box, box, box],
        out_specs=[box, stat_box, stat_box],
        out_shape=[jax.ShapeDtypeStruct(view(q).shape, F32), stat_shape, stat_shape],
        scratch_shapes=[pltpu.VMEM((chains * g, QB, d), BF16)] * 2,
        compiler_params=pltpu.CompilerParams(
            dimension_semantics=("arbitrary", "arbitrary", "arbitrary"),
            vmem_limit_bytes=VMEM_LIMIT_BYTES),
        name=f"attn_stage_d{dilation}",
    )(bias, view(q), view(k), view(v))
    return o.reshape(bsz * seq, d), m.reshape(bsz * seq, LANES), l.reshape(bsz * seq, LANES)


def _out_kernel(*refs, n_patterns):
    x_ref, ac_ref, za_ref, gc_ref, ga_ref = refs[:5]
    o_refs = refs[5:5 + n_patterns]
    m_refs = refs[5 + n_patterns:5 + 2 * n_patterns]
    l_refs = refs[5 + 2 * n_patterns:5 + 3 * n_patterns]
    spread_ref, perm_ref, woc_ref, woa_ref, wo_ref, fg_ref, y_ref = refs[5 + 3 * n_patterns:]

    ms = [r[...] for r in m_refs]
    top = functools.reduce(jnp.maximum, ms)
    ws = [jnp.exp2(m - top) for m in ms]
    total = functools.reduce(jnp.add, [w * r[...] for w, r in zip(ws, l_refs)])
    mixed_o = None
    for w, o_ref in zip(ws, o_refs):
        alpha = w / total
        hi = alpha.astype(BF16)
        lo = (alpha - hi.astype(F32)).astype(BF16)
        wide = jnp.dot(jnp.concatenate([hi, lo], axis=1), spread_ref[...],
                       preferred_element_type=F32)
        term = wide * o_ref[...]
        mixed_o = term if mixed_o is None else mixed_o + term
    gated = (za_ref[...].astype(F32) * mixed_o).astype(BF16)
    y_attn = jnp.dot(gated, woa_ref[...], preferred_element_type=F32)
    y_conv = jnp.dot(ac_ref[...], woc_ref[...], preferred_element_type=F32)
    merged = gc_ref[...].astype(F32) * y_conv + ga_ref[...].astype(F32) * y_attn
    merged = _permute_rows(perm_ref[...], merged.astype(BF16))
    mixed = jnp.dot(merged, wo_ref[...], preferred_element_type=F32)
    y_ref[...] = _rms_norm(x_ref[...] + mixed, fg_ref[...])


def _head_spread(d):
    s = np.zeros((2 * LANES, d), np.float32)
    for h in range(N_HEADS):
        s[h, h * HEAD_DIM:(h + 1) * HEAD_DIM] = 1.0
        s[LANES + h, h * HEAD_DIM:(h + 1) * HEAD_DIM] = 1.0
    return jnp.asarray(s, BF16)


def _out_stage(x, ac, za, gc, ga, os_, ms, ls, w_out_conv, w_out_attn, w_o, final_g):
    n, d = x.shape
    rows = OUT_ROWS
    n_patterns = len(os_)
    tile = pl.BlockSpec((rows, d), lambda i: (i, 0))
    stat = pl.BlockSpec((rows, LANES), lambda i: (i, 0))
    const = lambda shape: pl.BlockSpec(shape, lambda i: (0,) * len(shape),
                                       pipeline_mode=pl.Buffered(1))
    return pl.pallas_call(
        functools.partial(_out_kernel, n_patterns=n_patterns),
        grid=(n // rows,),
        in_specs=([tile] * (5 + n_patterns) + [stat] * (2 * n_patterns)
                  + [const((2 * LANES, d)), const((rows, rows))]
                  + [const((d, d))] * 3 + [const((1, d))]),
        out_specs=tile,
        out_shape=jax.ShapeDtypeStruct((n, d), F32),
        compiler_params=pltpu.CompilerParams(
            dimension_semantics=("arbitrary",),
            vmem_limit_bytes=VMEM_LIMIT_BYTES),
        name="out_stage",
    )(x, ac, za, gc, ga, *os_, *ms, *ls, _head_spread(d), _to_storage_order(rows).T,
      w_out_conv, w_out_attn, w_o, final_g)


def _layer(h, norm_g, w_in, b_merge, conv_w, w_out_conv, w_out_attn, w_o, out_g):
    bsz, seq, d = h.shape
    q, k, v, za, ac, gc, ga = _proj_stage(
        h, norm_g[None, :], w_in.astype(BF16), b_merge[None, :], conv_w)
    os_, ms, ls = zip(*[_attn_stage(q, k, v, dilation) for _, dilation in ATTN_PATTERNS])
    flat = lambda a: a.reshape(bsz * seq, d)
    y = _out_stage(flat(h), flat(ac), flat(za), flat(gc), flat(ga), os_, ms, ls,
                   w_out_conv.astype(BF16), w_out_attn.astype(BF16), w_o.astype(BF16),
                   out_g[None, :])
    return y.reshape(bsz, seq, d)


def kernel(x, norm_g, w_in, b_merge, conv_w, w_out_conv, w_out_attn, w_o, final_g):
    depth = norm_g.shape[0]
    assert depth == 1, "the fused output stage applies the final norm after a single layer"
    assert x.shape[-1] == N_HEADS * HEAD_DIM and x.shape[1] % 4096 == 0
    assert ATTN_PATTERNS == ((128, 1), (512, 4), (2048, 16)), "box specs are per pattern"
    return _layer(x, norm_g[0], w_in[0], b_merge[0], conv_w[0],
                  w_out_conv[0], w_out_attn[0], w_o[0], final_g)
```

```python
import functools

import jax
import jax.numpy as jnp
import numpy as np
from jax import lax
from jax.experimental import pallas as pl
from jax.experimental.pallas import tpu as pltpu

N_HEADS = 16
HEAD_DIM = 64
CONV_WIDTH = 3
ATTN_PATTERNS = ((128, 1), (512, 4), (2048, 16))
QB = 128
EPS = 1e-6
LOG2_E = 1.4426950408889634

LANES = 128
SUBLANES = 8
HEADS_PER_LANE_TILE = LANES // HEAD_DIM
GROUPS = QB // SUBLANES
ATTN_BLOCKS = 8
PROJ_ROWS = 256
OUT_ROWS = 256
VMEM_LIMIT_BYTES = 56 * 1024 * 1024

BF16 = jnp.bfloat16
F32 = jnp.float32


def _rms_norm(x, g):
    return x * lax.rsqrt(jnp.mean(x * x, axis=-1, keepdims=True) + EPS) * g


def _to_storage_order(rows):
    s = np.arange(rows)
    natural = (s // QB) * QB + (s % SUBLANES) * GROUPS + (s % QB) // SUBLANES
    p = np.zeros((rows, rows), np.float32)
    p[s, natural] = 1.0
    return jnp.asarray(p, BF16)


def _permute_rows(p, a):
    return jnp.dot(p, a, preferred_element_type=F32).astype(BF16)


def _previous_token(a, carry_group):
    rows, d = a.shape
    first_row = lax.broadcasted_iota(jnp.int32, (SUBLANES, d), 0) == 0
    pieces = []
    prev_last = carry_group
    for blk in range(rows // QB):
        base = blk * QB
        last = a[base + QB - SUBLANES:base + QB, :]
        pieces.append(jnp.where(first_row, pltpu.roll(prev_last, 1, 0), pltpu.roll(last, 1, 0)))
        pieces.append(a[base:base + QB - SUBLANES, :])
        prev_last = last
    return jnp.concatenate(pieces, axis=0)


def _proj_kernel(x_ref, perm_ref, g_ref, w_ref, b_ref, cw_ref,
                 q_ref, k_ref, v_ref, za_ref, ac_ref, gc_ref, ga_ref, carry):
    rows, d = x_ref.shape
    u = _permute_rows(perm_ref[...], _rms_norm(x_ref[...], g_ref[...]).astype(BF16))

    def proj(c):
        return jnp.dot(u, w_ref[:, c * d:(c + 1) * d], preferred_element_type=F32)

    @pl.when(pl.program_id(1) == 0)
    def _():
        carry[...] = jnp.zeros(carry.shape, F32)

    a = proj(2) * proj(0)
    back1 = _previous_token(a, carry[SUBLANES:2 * SUBLANES, :])
    back2 = _previous_token(back1, carry[0:SUBLANES, :])
    carry[...] = a[rows - 2 * SUBLANES:rows, :]
    c = cw_ref[2:3, :] * a + cw_ref[1:2, :] * back1 + cw_ref[0:1, :] * back2
    ac_ref[...] = (jax.nn.silu(proj(3)) * proj(1) * c).astype(BF16)

    q_ref[...] = proj(4) * (HEAD_DIM ** -0.5 * LOG2_E)
    k_ref[...] = proj(5)
    v_ref[...] = proj(6)
    za_ref[...] = jax.nn.silu(proj(7)).astype(BF16)
    gc_ref[...] = jax.nn.sigmoid(proj(8) + b_ref[:, 0:d]).astype(BF16)
    ga_ref[...] = jax.nn.sigmoid(proj(9) + b_ref[:, d:2 * d]).astype(BF16)


def _proj_stage(x, norm_g, w_in, b_merge, conv_w):
    bsz, seq, d = x.shape
    rows = PROJ_ROWS
    tile = pl.BlockSpec((None, rows, d), lambda b, i: (b, i, 0))
    const = lambda shape: pl.BlockSpec(shape, lambda b, i: (0,) * len(shape),
                                       pipeline_mode=pl.Buffered(1))
    wide = jax.ShapeDtypeStruct((bsz, seq, d), F32)
    narrow = jax.ShapeDtypeStruct((bsz, seq, d), BF16)
    return pl.pallas_call(
        _proj_kernel,
        grid=(bsz, seq // rows),
        in_specs=[tile, const((rows, rows)), const((1, d)), const(w_in.shape),
                  const((1, 2 * d)), const((CONV_WIDTH, d))],
        out_specs=[tile] * 7,
        out_shape=[wide] * 3 + [narrow] * 4,
        scratch_shapes=[pltpu.VMEM((2 * SUBLANES, d), F32)],
        compiler_params=pltpu.CompilerParams(
            dimension_semantics=("arbitrary", "arbitrary"),
            vmem_limit_bytes=VMEM_LIMIT_BYTES),
        name="proj_stage",
    )(x, _to_storage_order(rows), norm_g, w_in, b_merge, conv_w)


def _attn_kernel(bias_ref, q_ref, k_ref, v_ref, o_ref, m_ref, l_ref, kbuf, vbuf, *,
                 lead, chained):
    n_blocks, d = int(np.prod(lead)), q_ref.shape[-1]
    n_chains = q_ref.shape[-3] if chained else 1
    step = pl.program_id(2)
    first = (step == 0).astype(jnp.int32)

    @pl.when(step == 0)
    def _():
        for c in range(n_chains):
            kbuf[c * n_blocks] = jnp.zeros((QB, d), BF16)
            vbuf[c * n_blocks] = jnp.zeros((QB, d), BF16)

    lane = lax.broadcasted_iota(jnp.int32, (QB, LANES), 1)
    head_masks = [(lane // HEAD_DIM) == e for e in range(HEADS_PER_LANE_TILE)]
    rows = (slice(None),) * 2

    for c, g in [(c, g) for c in range(n_chains) for g in range(n_blocks)]:
        chain = (c,) if chained else ()
        where = tuple(int(i) for i in np.unravel_index(g, lead))
        slot = c * n_blocks + g
        keep = c * n_blocks + (g + 1) % n_blocks
        m_tile = jnp.zeros((QB, LANES), F32)
        l_tile = jnp.ones((QB, LANES), F32)
        for t in range(N_HEADS // HEADS_PER_LANE_TILE):
            cols = slice(t * LANES, (t + 1) * LANES)
            at = where + rows + chain + (slice(None), cols)
            block = lambda ref: ref[at].reshape(QB, LANES).astype(BF16)
            q, k_cur, v_cur = block(q_ref), block(k_ref), block(v_ref)
            k = jnp.concatenate([kbuf[slot, :, cols], k_cur], axis=0)
            v = jnp.concatenate([vbuf[slot, :, cols], v_cur], axis=0)
            kbuf[keep, :, cols] = k_cur
            vbuf[keep, :, cols] = v_cur
            q_heads = jnp.concatenate(
                [jnp.where(m, q, jnp.zeros_like(q)) for m in head_masks], axis=0)
            s_all = lax.dot_general(q_heads, k, (((1,), (1,)), ((), ())),
                                    preferred_element_type=F32)
            ps = []
            for e in range(HEADS_PER_LANE_TILE):
                h = t * HEADS_PER_LANE_TILE + e
                s = s_all[e * QB:(e + 1) * QB, :] + bias_ref[first if g == 0 else 0, h]
                m = jnp.max(s, axis=-1, keepdims=True)
                p = jnp.exp2(s - m)
                m_tile = jnp.where(lane == h, m, m_tile)
                l_tile = jnp.where(lane == h, jnp.sum(p, axis=-1, keepdims=True), l_tile)
                ps.append(p.astype(BF16))
            acc_all = jnp.dot(jnp.concatenate(ps, axis=0), v, preferred_element_type=F32)
            o_tile = acc_all[0:QB, :]
            for e in range(1, HEADS_PER_LANE_TILE):
                o_tile = jnp.where(head_masks[e], acc_all[e * QB:(e + 1) * QB, :], o_tile)
            o_ref[at] = o_tile.reshape(4, 4, SUBLANES, LANES)
        stat_at = where + rows + chain
        m_ref[stat_at] = m_tile.reshape(4, 4, SUBLANES, LANES)
        l_ref[stat_at] = l_tile.reshape(4, 4, SUBLANES, LANES)


def _block_positions(dilation):
    row = np.arange(QB)
    if dilation == 16:
        return row
    if dilation == 4:
        return (row // 32) * 32 + (row % 8) * 4 + (row // 8) % 4
    if dilation == 1:
        return (row % 8) * 16 + row // 8
    raise NotImplementedError(dilation)


def _alibi_bias(dilation):
    pos = _block_positions(dilation)
    q_loc = pos[:, None] + QB
    k_loc = np.concatenate([pos, pos + QB])[None, :]
    delta = q_loc - k_loc
    valid = (delta >= 0) & (delta <= QB)
    has_prev = np.stack([np.ones((2 * QB,), bool), np.arange(2 * QB) >= QB])
    ok = jnp.asarray(valid[None, None] & has_prev[:, None, None, :])
    slopes = jnp.exp2(-8.0 * jnp.arange(1, N_HEADS + 1, dtype=F32) / N_HEADS)
    bias = -slopes[:, None, None] * jnp.asarray(dilation * delta, F32)[None]
    return jnp.where(ok, bias[None] * LOG2_E, -jnp.inf)


def _box_spec(dilation, width, g, chains):
    if dilation == 16 and chains > 1:
        low = 4 // chains
        return pl.BlockSpec((None, g, 4, 4, None, chains, SUBLANES, width),
                            lambda b, r, i: (b, i, 0, 0, r // low, r % low, 0, 0))
    assert chains == 1
    if dilation == 16:
        return pl.BlockSpec((None, g, 4, 4, None, None, SUBLANES, width),
                            lambda b, r, i: (b, i, 0, 0, r // 4, r % 4, 0, 0))
    if g > 4:
        hi = g // 4
        if dilation == 4:
            return pl.BlockSpec((None, hi, 4, 4, 4, None, SUBLANES, width),
                                lambda b, r, i: (b, i, 0, 0, 0, r, 0, 0))
        if dilation == 1:
            per = 4 // hi
            return pl.BlockSpec((None, None, hi, 4, 4, 4, SUBLANES, width),
                                lambda b, r, i: (b, i // per, i % per, 0, 0, 0, 0, 0))
    per = 4 // g
    if dilation == 4:
        return pl.BlockSpec((None, None, g, 4, 4, None, SUBLANES, width),
                            lambda b, r, i: (b, i // per, i % per, 0, 0, r, 0, 0))
    if dilation == 1:
        return pl.BlockSpec((None, None, None, g, 4, 4, SUBLANES, width),
                            lambda b, r, i: (b, i // (4 * per), (i // per) % 4, i % per, 0, 0, 0, 0))
    raise NotImplementedError(dilation)


def _attn_stage(q, k, v, dilation):
    bsz, seq, d = q.shape
    nb = seq // (dilation * QB)
    view = lambda a: a.reshape(bsz, seq // 2048, 4, 4, 4, 4, SUBLANES, a.shape[-1])
    bias = _alibi_bias(dilation)
    bias_spec = pl.BlockSpec(bias.shape, lambda b, r, i: (0, 0, 0, 0),
                             pipeline_mode=pl.Buffered(1))
    g = min(ATTN_BLOCKS, nb)
    chains = min(ATTN_BLOCKS // g, 4) if dilation == 16 else 1
    lead = (g,) if g <= 4 or dilation == 16 else (g // 4, 4)
    box = _box_spec(dilation, d, g, chains)
    stat_box = _box_spec(dilation, LANES, g, chains)
    stat_shape = jax.ShapeDtypeStruct(view(q).shape[:-1] + (LANES,), F32)
    o, m, l = pl.pallas_call(
        functools.partial(_attn_kernel, lead=lead, chained=chains > 1),
        grid=(bsz, dilation // chains, nb // g),
        in_specs=[bias_spec, box, box, box],
        out_specs=[box, stat_box, stat_box],
        out_shape=[jax.ShapeDtypeStruct(view(q).shape, F32), stat_shape, stat_shape],
        scratch_shapes=[pltpu.VMEM((chains * g, QB, d), BF16)] * 2,
        compiler_params=pltpu.CompilerParams(
            dimension_semantics=("arbitrary", "arbitrary", "arbitrary"),
            vmem_limit_bytes=VMEM_LIMIT_BYTES),
        name=f"attn_stage_d{dilation}",
    )(bias, view(q), view(k), view(v))
    return o.reshape(bsz * seq, d), m.reshape(bsz * seq, LANES), l.reshape(bsz * seq, LANES)


def _out_kernel(*refs, n_patterns):
    x_ref, ac_ref, za_ref, gc_ref, ga_ref = refs[:5]
    o_refs = refs[5:5 + n_patterns]
    m_refs = refs[5 + n_patterns:5 + 2 * n_patterns]
    l_refs = refs[5 + 2 * n_patterns:5 + 3 * n_patterns]
    spread_ref, perm_ref, woc_ref, woa_ref, wo_ref, fg_ref, y_ref = refs[5 + 3 * n_patterns:]

    ms = [r[...] for r in m_refs]
    top = functools.reduce(jnp.maximum, ms)
    ws = [jnp.exp2(m - top) for m in ms]
    total = functools.reduce(jnp.add, [w * r[...] for w, r in zip(ws, l_refs)])
    mixed_o = None
    for w, o_ref in zip(ws, o_refs):
        alpha = w / total
        hi = alpha.astype(BF16)
        lo = (alpha - hi.astype(F32)).astype(BF16)
        wide = jnp.dot(jnp.concatenate([hi, lo], axis=1), spread_ref[...],
                       preferred_element_type=F32)
        term = wide * o_ref[...]
        mixed_o = term if mixed_o is None else mixed_o + term
    gated = (za_ref[...].astype(F32) * mixed_o).astype(BF16)
    y_attn = jnp.dot(gated, woa_ref[...], preferred_element_type=F32)
    y_conv = jnp.dot(ac_ref[...], woc_ref[...], preferred_element_type=F32)
    merged = gc_ref[...].astype(F32) * y_conv + ga_ref[...].astype(F32) * y_attn
    merged = _permute_rows(perm_ref[...], merged.astype(BF16))
    mixed = jnp.dot(merged, wo_ref[...], preferred_element_type=F32)
    y_ref[...] = _rms_norm(x_ref[...] + mixed, fg_ref[...])


def _head_spread(d):
    s = np.zeros((2 * LANES, d), np.float32)
    for h in range(N_HEADS):
        s[h, h * HEAD_DIM:(h + 1) * HEAD_DIM] = 1.0
        s[LANES + h, h * HEAD_DIM:(h + 1) * HEAD_DIM] = 1.0
    return jnp.asarray(s, BF16)


def _out_stage(x, ac, za, gc, ga, os_, ms, ls, w_out_conv, w_out_attn, w_o, final_g):
    n, d = x.shape
    rows = OUT_ROWS
    n_patterns = len(os_)
    tile = pl.BlockSpec((rows, d), lambda i: (i, 0))
    stat = pl.BlockSpec((rows, LANES), lambda i: (i, 0))
    const = lambda shape: pl.BlockSpec(shape, lambda i: (0,) * len(shape),
                                       pipeline_mode=pl.Buffered(1))
    return pl.pallas_call(
        functools.partial(_out_kernel, n_patterns=n_patterns),
        grid=(n // rows,),
        in_specs=([tile] * (5 + n_patterns) + [stat] * (2 * n_patterns)
                  + [const((2 * LANES, d)), const((rows, rows))]
                  + [const((d, d))] * 3 + [const((1, d))]),
        out_specs=tile,
        out_shape=jax.ShapeDtypeStruct((n, d), F32),
        compiler_params=pltpu.CompilerParams(
            dimension_semantics=("arbitrary",),
            vmem_limit_bytes=VMEM_LIMIT_BYTES),
        name="out_stage",
    )(x, ac, za, gc, ga, *os_, *ms, *ls, _head_spread(d), _to_storage_order(rows).T,
      w_out_conv, w_out_attn, w_o, final_g)


def _layer(h, norm_g, w_in, b_merge, conv_w, w_out_conv, w_out_attn, w_o, out_g):
    bsz, seq, d = h.shape
    q, k, v, za, ac, gc, ga = _proj_stage(
        h, norm_g[None, :], w_in.astype(BF16), b_merge[None, :], conv_w)
    os_, ms, ls = zip(*[_attn_stage(q, k, v, dilation) for _, dilation in ATTN_PATTERNS])
    flat = lambda a: a.reshape(bsz * seq, d)
    y = _out_stage(flat(h), flat(ac), flat(za), flat(gc), flat(ga), os_, ms, ls,
                   w_out_conv.astype(BF16), w_out_attn.astype(BF16), w_o.astype(BF16),
                   out_g[None, :])
    return y.reshape(bsz, seq, d)


def kernel(x, norm_g, w_in, b_merge, conv_w, w_out_conv, w_out_attn, w_o, final_g):
    depth = norm_g.shape[0]
    assert depth == 1, "the fused output stage applies the final norm after a single layer"
    assert x.shape[-1] == N_HEADS * HEAD_DIM and x.shape[1] % 4096 == 0
    assert ATTN_PATTERNS == ((128, 1), (512, 4), (2048, 16)), "box specs are per pattern"
    return _layer(x, norm_g[0], w_in[0], b_merge[0], conv_w[0],
                  w_out_conv[0], w_out_attn[0], w_o[0], final_g)
```

```python
import functools

import jax
import jax.numpy as jnp
import numpy as np
from jax import lax
from jax.experimental import pallas as pl
from jax.experimental.pallas import tpu as pltpu

N_HEADS = 16
HEAD_DIM = 64
CONV_WIDTH = 3
ATTN_PATTERNS = ((128, 1), (512, 4), (2048, 16))
QB = 128
EPS = 1e-6
LOG2_E = 1.4426950408889634

LANES = 128
SUBLANES = 8
HEADS_PER_LANE_TILE = LANES // HEAD_DIM
GROUPS = QB // SUBLANES
ATTN_BLOCKS = 8
PROJ_ROWS = 256
OUT_ROWS = 256
VMEM_LIMIT_BYTES = 56 * 1024 * 1024

BF16 = jnp.bfloat16
F32 = jnp.float32


def _rms_norm(x, g):
    return x * lax.rsqrt(jnp.mean(x * x, axis=-1, keepdims=True) + EPS) * g


def _to_storage_order(rows):
    s = np.arange(rows)
    natural = (s // QB) * QB + (s % SUBLANES) * GROUPS + (s % QB) // SUBLANES
    p = np.zeros((rows, rows), np.float32)
    p[s, natural] = 1.0
    return jnp.asarray(p, BF16)


def _permute_rows(p, a):
    return jnp.dot(p, a, preferred_element_type=F32).astype(BF16)


def _previous_token(a, carry_group):
    rows, d = a.shape
    first_row = lax.broadcasted_iota(jnp.int32, (SUBLANES, d), 0) == 0
    pieces = []
    prev_last = carry_group
    for blk in range(rows // QB):
        base = blk * QB
        last = a[base + QB - SUBLANES:base + QB, :]
        pieces.append(jnp.where(first_row, pltpu.roll(prev_last, 1, 0), pltpu.roll(last, 1, 0)))
        pieces.append(a[base:base + QB - SUBLANES, :])
        prev_last = last
    return jnp.concatenate(pieces, axis=0)


def _proj_kernel(x_ref, perm_ref, g_ref, w_ref, b_ref, cw_ref,
                 q_ref, k_ref, v_ref, za_ref, ac_ref, gc_ref, ga_ref, carry):
    rows, d = x_ref.shape
    u = _permute_rows(perm_ref[...], _rms_norm(x_ref[...], g_ref[...]).astype(BF16))

    def proj(c):
        return jnp.dot(u, w_ref[:, c * d:(c + 1) * d], preferred_element_type=F32)

    @pl.when(pl.program_id(1) == 0)
    def _():
        carry[...] = jnp.zeros(carry.shape, F32)

    a = proj(2) * proj(0)
    back1 = _previous_token(a, carry[SUBLANES:2 * SUBLANES, :])
    back2 = _previous_token(back1, carry[0:SUBLANES, :])
    carry[...] = a[rows - 2 * SUBLANES:rows, :]
    c = cw_ref[2:3, :] * a + cw_ref[1:2, :] * back1 + cw_ref[0:1, :] * back2
    ac_ref[...] = (jax.nn.silu(proj(3)) * proj(1) * c).astype(BF16)

    q_ref[...] = proj(4) * (HEAD_DIM ** -0.5 * LOG2_E)
    k_ref[...] = proj(5)
    v_ref[...] = proj(6)
    za_ref[...] = jax.nn.silu(proj(7)).astype(BF16)
    gc_ref[...] = jax.nn.sigmoid(proj(8) + b_ref[:, 0:d]).astype(BF16)
    ga_ref[...] = jax.nn.sigmoid(proj(9) + b_ref[:, d:2 * d]).astype(BF16)


def _proj_stage(x, norm_g, w_in, b_merge, conv_w):
    bsz, seq, d = x.shape
    rows = PROJ_ROWS
    tile = pl.BlockSpec((None, rows, d), lambda b, i: (b, i, 0))
    const = lambda shape: pl.BlockSpec(shape, lambda b, i: (0,) * len(shape),
                                       pipeline_mode=pl.Buffered(1))
    wide = jax.ShapeDtypeStruct((bsz, seq, d), F32)
    narrow = jax.ShapeDtypeStruct((bsz, seq, d), BF16)
    return pl.pallas_call(
        _proj_kernel,
        grid=(bsz, seq // rows),
        in_specs=[tile, const((rows, rows)), const((1, d)), const(w_in.shape),
                  const((1, 2 * d)), const((CONV_WIDTH, d))],
        out_specs=[tile] * 7,
        out_shape=[wide] * 3 + [narrow] * 4,
        scratch_shapes=[pltpu.VMEM((2 * SUBLANES, d), F32)],
        compiler_params=pltpu.CompilerParams(
            dimension_semantics=("arbitrary", "arbitrary"),
            vmem_limit_bytes=VMEM_LIMIT_BYTES),
        name="proj_stage",
    )(x, _to_storage_order(rows), norm_g, w_in, b_merge, conv_w)


def _attn_kernel(bias_ref, q_ref, k_ref, v_ref, o_ref, stat_ref, kbuf, vbuf, *,
                 lead, chained):
    n_blocks, d = int(np.prod(lead)), q_ref.shape[-1]
    n_chains = q_ref.shape[-3] if chained else 1
    step = pl.program_id(2)
    first = (step == 0).astype(jnp.int32)

    @pl.when(step == 0)
    def _():
        for c in range(n_chains):
            kbuf[c * n_blocks] = jnp.zeros((QB, d), BF16)
            vbuf[c * n_blocks] = jnp.zeros((QB, d), BF16)

    lane = lax.broadcasted_iota(jnp.int32, (QB, LANES), 1)
    head_masks = [(lane // HEAD_DIM) == e for e in range(HEADS_PER_LANE_TILE)]
    rows = (slice(None),) * 2

    for c, g in [(c, g) for c in range(n_chains) for g in range(n_blocks)]:
        chain = (c,) if chained else ()
        where = tuple(int(i) for i in np.unravel_index(g, lead))
        slot = c * n_blocks + g
        keep = c * n_blocks + (g + 1) % n_blocks
        stat_tile = jnp.zeros((QB, LANES), F32)
        for t in range(N_HEADS // HEADS_PER_LANE_TILE):
            cols = slice(t * LANES, (t + 1) * LANES)
            at = where + rows + chain + (slice(None), cols)
            block = lambda ref: ref[at].reshape(QB, LANES).astype(BF16)
            q, k_cur, v_cur = block(q_ref), block(k_ref), block(v_ref)
            k = jnp.concatenate([kbuf[slot, :, cols], k_cur], axis=0)
            v = jnp.concatenate([vbuf[slot, :, cols], v_cur], axis=0)
            kbuf[keep, :, cols] = k_cur
            vbuf[keep, :, cols] = v_cur
            q_heads = jnp.concatenate(
                [jnp.where(m, q, jnp.zeros_like(q)) for m in head_masks], axis=0)
            s_all = lax.dot_general(q_heads, k, (((1,), (1,)), ((), ())),
                                    preferred_element_type=F32)
            ps = []
            for e in range(HEADS_PER_LANE_TILE):
                h = t * HEADS_PER_LANE_TILE + e
                s = s_all[e * QB:(e + 1) * QB, :] + bias_ref[first if g == 0 else 0, h]
                m = jnp.max(s, axis=-1, keepdims=True)
                p = jnp.exp2(s - m)
                stat_tile = jnp.where(lane == h, m, stat_tile)
                stat_tile = jnp.where(lane == N_HEADS + h,
                                      jnp.sum(p, axis=-1, keepdims=True), stat_tile)
                ps.append(p.astype(BF16))
            acc_all = jnp.dot(jnp.concatenate(ps, axis=0), v, preferred_element_type=F32)
            o_tile = acc_all[0:QB, :]
            for e in range(1, HEADS_PER_LANE_TILE):
                o_tile = jnp.where(head_masks[e], acc_all[e * QB:(e + 1) * QB, :], o_tile)
            o_ref[at] = o_tile.reshape(4, 4, SUBLANES, LANES)
        stat_at = where + rows + chain
        stat_ref[stat_at] = stat_tile.reshape(4, 4, SUBLANES, LANES)


def _block_positions(dilation):
    row = np.arange(QB)
    if dilation == 16:
        return row
    if dilation == 4:
        return (row // 32) * 32 + (row % 8) * 4 + (row // 8) % 4
    if dilation == 1:
        return (row % 8) * 16 + row // 8
    raise NotImplementedError(dilation)


def _alibi_bias(dilation):
    pos = _block_positions(dilation)
    q_loc = pos[:, None] + QB
    k_loc = np.concatenate([pos, pos + QB])[None, :]
    delta = q_loc - k_loc
    valid = (delta >= 0) & (delta <= QB)
    has_prev = np.stack([np.ones((2 * QB,), bool), np.arange(2 * QB) >= QB])
    ok = jnp.asarray(valid[None, None] & has_prev[:, None, None, :])
    slopes = jnp.exp2(-8.0 * jnp.arange(1, N_HEADS + 1, dtype=F32) / N_HEADS)
    bias = -slopes[:, None, None] * jnp.asarray(dilation * delta, F32)[None]
    return jnp.where(ok, bias[None] * LOG2_E, -jnp.inf)


def _box_spec(dilation, width, g, chains):
    if dilation == 16 and chains > 1:
        low = 4 // chains
        return pl.BlockSpec((None, g, 4, 4, None, chains, SUBLANES, width),
                            lambda b, r, i: (b, i, 0, 0, r // low, r % low, 0, 0))
    assert chains == 1
    if dilation == 16:
        return pl.BlockSpec((None, g, 4, 4, None, None, SUBLANES, width),
                            lambda b, r, i: (b, i, 0, 0, r // 4, r % 4, 0, 0))
    if g > 4:
        hi = g // 4
        if dilation == 4:
            return pl.BlockSpec((None, hi, 4, 4, 4, None, SUBLANES, width),
                                lambda b, r, i: (b, i, 0, 0, 0, r, 0, 0))
        if dilation == 1:
            per = 4 // hi
            return pl.BlockSpec((None, None, hi, 4, 4, 4, SUBLANES, width),
                                lambda b, r, i: (b, i // per, i % per, 0, 0, 0, 0, 0))
    per = 4 // g
    if dilation == 4:
        return pl.BlockSpec((None, None, g, 4, 4, None, SUBLANES, width),
                            lambda b, r, i: (b, i // per, i % per, 0, 0, r, 0, 0))
    if dilation == 1:
        return pl.BlockSpec((None, None, None, g, 4, 4, SUBLANES, width),
                            lambda b, r, i: (b, i // (4 * per), (i // per) % 4, i % per, 0, 0, 0, 0))
    raise NotImplementedError(dilation)


def _attn_stage(q, k, v, dilation):
    bsz, seq, d = q.shape
    nb = seq // (dilation * QB)
    view = lambda a: a.reshape(bsz, seq // 2048, 4, 4, 4, 4, SUBLANES, a.shape[-1])
    bias = _alibi_bias(dilation)
    bias_spec = pl.BlockSpec(bias.shape, lambda b, r, i: (0, 0, 0, 0),
                             pipeline_mode=pl.Buffered(1))
    g = min(ATTN_BLOCKS, nb)
    chains = min(ATTN_BLOCKS // g, 4) if dilation == 16 else 1
    lead = (g,) if g <= 4 or dilation == 16 else (g // 4, 4)
    box = _box_spec(dilation, d, g, chains)
    stat_box = _box_spec(dilation, LANES, g, chains)
    stat_shape = jax.ShapeDtypeStruct(view(q).shape[:-1] + (LANES,), F32)
    o, stat = pl.pallas_call(
        functools.partial(_attn_kernel, lead=lead, chained=chains > 1),
        grid=(bsz, dilation // chains, nb // g),
        in_specs=[bias_spec, box, box, box],
        out_specs=[box, stat_box],
        out_shape=[jax.ShapeDtypeStruct(view(q).shape, F32), stat_shape],
        scratch_shapes=[pltpu.VMEM((chains * g, QB, d), BF16)] * 2,
        compiler_params=pltpu.CompilerParams(
            dimension_semantics=("arbitrary", "arbitrary", "arbitrary"),
            vmem_limit_bytes=VMEM_LIMIT_BYTES),
        name=f"attn_stage_d{dilation}",
    )(bias, view(q), view(k), view(v))
    return o.reshape(bsz * seq, d), stat.reshape(bsz * seq, LANES)


def _out_kernel(*refs, n_patterns):
    x_ref, ac_ref, za_ref, gc_ref, ga_ref = refs[:5]
    o_refs = refs[5:5 + n_patterns]
    stat_refs = refs[5 + n_patterns:5 + 2 * n_patterns]
    spread_ref, perm_ref, woc_ref, woa_ref, wo_ref, fg_ref, y_ref = refs[5 + 2 * n_patterns:]

    ms = [r[...] for r in stat_refs]
    lane = lax.broadcasted_iota(jnp.int32, ms[0].shape, 1)
    ls = [jnp.where(lane < N_HEADS, pltpu.roll(m, LANES - N_HEADS, 1), 1.0) for m in ms]
    top = functools.reduce(jnp.maximum, ms)
    ws = [jnp.exp2(m - top) for m in ms]
    total = functools.reduce(jnp.add, [w * l for w, l in zip(ws, ls)])
    mixed_o = None
    for w, o_ref in zip(ws, o_refs):
        alpha = w / total
        hi = alpha.astype(BF16)
        lo = (alpha - hi.astype(F32)).astype(BF16)
        wide = jnp.dot(jnp.concatenate([hi, lo], axis=1), spread_ref[...],
                       preferred_element_type=F32)
        term = wide * o_ref[...]
        mixed_o = term if mixed_o is None else mixed_o + term
    gated = (za_ref[...].astype(F32) * mixed_o).astype(BF16)
    y_attn = jnp.dot(gated, woa_ref[...], preferred_element_type=F32)
    y_conv = jnp.dot(ac_ref[...], woc_ref[...], preferred_element_type=F32)
    merged = gc_ref[...].astype(F32) * y_conv + ga_ref[...].astype(F32) * y_attn
    merged = _permute_rows(perm_ref[...], merged.astype(BF16))
    mixed = jnp.dot(merged, wo_ref[...], preferred_element_type=F32)
    y_ref[...] = _rms_norm(x_ref[...] + mixed, fg_ref[...])


def _head_spread(d):
    s = np.zeros((2 * LANES, d), np.float32)
    for h in range(N_HEADS):
        s[h, h * HEAD_DIM:(h + 1) * HEAD_DIM] = 1.0
        s[LANES + h, h * HEAD_DIM:(h + 1) * HEAD_DIM] = 1.0
    return jnp.asarray(s, BF16)


def _out_stage(x, ac, za, gc, ga, os_, stats, w_out_conv, w_out_attn, w_o, final_g):
    n, d = x.shape
    rows = OUT_ROWS
    n_patterns = len(os_)
    tile = pl.BlockSpec((rows, d), lambda i: (i, 0))
    stat = pl.BlockSpec((rows, LANES), lambda i: (i, 0))
    const = lambda shape: pl.BlockSpec(shape, lambda i: (0,) * len(shape),
                                       pipeline_mode=pl.Buffered(1))
    return pl.pallas_call(
        functools.partial(_out_kernel, n_patterns=n_patterns),
        grid=(n // rows,),
        in_specs=([tile] * (5 + n_patterns) + [stat] * n_patterns
                  + [const((2 * LANES, d)), const((rows, rows))]
                  + [const((d, d))] * 3 + [const((1, d))]),
        out_specs=tile,
        out_shape=jax.ShapeDtypeStruct((n, d), F32),
        compiler_params=pltpu.CompilerParams(
            dimension_semantics=("arbitrary",),
            vmem_limit_bytes=VMEM_LIMIT_BYTES),
        name="out_stage",
    )(x, ac, za, gc, ga, *os_, *stats, _head_spread(d), _to_storage_order(rows).T,
      w_out_conv, w_out_attn, w_o, final_g)


def _layer(h, norm_g, w_in, b_merge, conv_w, w_out_conv, w_out_attn, w_o, out_g):
    bsz, seq, d = h.shape
    q, k, v, za, ac, gc, ga = _proj_stage(
        h, norm_g[None, :], w_in.astype(BF16), b_merge[None, :], conv_w)
    os_, stats = zip(*[_attn_stage(q, k, v, dilation) for _, dilation in ATTN_PATTERNS])
    flat = lambda a: a.reshape(bsz * seq, d)
    y = _out_stage(flat(h), flat(ac), flat(za), flat(gc), flat(ga), os_, stats,
                   w_out_conv.astype(BF16), w_out_attn.astype(BF16), w_o.astype(BF16),
                   out_g[None, :])
    return y.reshape(bsz, seq, d)


def kernel(x, norm_g, w_in, b_merge, conv_w, w_out_conv, w_out_attn, w_o, final_g):
    depth = norm_g.shape[0]
    assert depth == 1, "the fused output stage applies the final norm after a single layer"
    assert x.shape[-1] == N_HEADS * HEAD_DIM and x.shape[1] % 4096 == 0
    assert ATTN_PATTERNS == ((128, 1), (512, 4), (2048, 16)), "box specs are per pattern"
    return _layer(x, norm_g[0], w_in[0], b_merge[0], conv_w[0],
                  w_out_conv[0], w_out_attn[0], w_o[0], final_g)
```

```python
import functools

import jax
import jax.numpy as jnp
import numpy as np
from jax import lax
from jax.experimental import pallas as pl
from jax.experimental.pallas import tpu as pltpu

N_HEADS = 16
HEAD_DIM = 64
CONV_WIDTH = 3
ATTN_PATTERNS = ((128, 1), (512, 4), (2048, 16))
QB = 128
EPS = 1e-6
LOG2_E = 1.4426950408889634

LANES = 128
SUBLANES = 8
HEADS_PER_LANE_TILE = LANES // HEAD_DIM
GROUPS = QB // SUBLANES
ATTN_BLOCKS = 8
PERM_ROWS = 256
PROJ_ROWS = 256
OUT_ROWS = 512
VMEM_LIMIT_BYTES = 56 * 1024 * 1024

BF16 = jnp.bfloat16
F32 = jnp.float32


def _rms_norm(x, g):
    return x * lax.rsqrt(jnp.mean(x * x, axis=-1, keepdims=True) + EPS) * g


def _to_storage_order(rows):
    s = np.arange(rows)
    natural = (s // QB) * QB + (s % SUBLANES) * GROUPS + (s % QB) // SUBLANES
    p = np.zeros((rows, rows), np.float32)
    p[s, natural] = 1.0
    return jnp.asarray(p, BF16)


def _permute_rows(p, a):
    n = p.shape[0]
    return jnp.concatenate(
        [jnp.dot(p, a[i:i + n, :], preferred_element_type=F32).astype(BF16)
         for i in range(0, a.shape[0], n)], axis=0)


def _previous_token(a, carry_group):
    rows, d = a.shape
    first_row = lax.broadcasted_iota(jnp.int32, (SUBLANES, d), 0) == 0
    pieces = []
    prev_last = carry_group
    for blk in range(rows // QB):
        base = blk * QB
        last = a[base + QB - SUBLANES:base + QB, :]
        pieces.append(jnp.where(first_row, pltpu.roll(prev_last, 1, 0), pltpu.roll(last, 1, 0)))
        pieces.append(a[base:base + QB - SUBLANES, :])
        prev_last = last
    return jnp.concatenate(pieces, axis=0)


def _proj_kernel(x_ref, perm_ref, g_ref, w_ref, b_ref, cw_ref,
                 q_ref, k_ref, v_ref, za_ref, ac_ref, gc_ref, ga_ref, carry):
    rows, d = x_ref.shape
    u = _permute_rows(perm_ref[...], _rms_norm(x_ref[...], g_ref[...]).astype(BF16))

    def proj(c):
        return jnp.dot(u, w_ref[:, c * d:(c + 1) * d], preferred_element_type=F32)

    @pl.when(pl.program_id(1) == 0)
    def _():
        carry[...] = jnp.zeros(carry.shape, F32)

    a = proj(2) * proj(0)
    back1 = _previous_token(a, carry[SUBLANES:2 * SUBLANES, :])
    back2 = _previous_token(back1, carry[0:SUBLANES, :])
    carry[...] = a[rows - 2 * SUBLANES:rows, :]
    c = cw_ref[2:3, :] * a + cw_ref[1:2, :] * back1 + cw_ref[0:1, :] * back2
    ac_ref[...] = (jax.nn.silu(proj(3)) * proj(1) * c).astype(BF16)

    q_ref[...] = proj(4) * (HEAD_DIM ** -0.5 * LOG2_E)
    k_ref[...] = proj(5)
    v_ref[...] = proj(6)
    za_ref[...] = jax.nn.silu(proj(7)).astype(BF16)
    gc_ref[...] = jax.nn.sigmoid(proj(8) + b_ref[:, 0:d]).astype(BF16)
    ga_ref[...] = jax.nn.sigmoid(proj(9) + b_ref[:, d:2 * d]).astype(BF16)


def _proj_stage(x, norm_g, w_in, b_merge, conv_w):
    bsz, seq, d = x.shape
    rows = PROJ_ROWS
    tile = pl.BlockSpec((None, rows, d), lambda b, i: (b, i, 0))
    const = lambda shape: pl.BlockSpec(shape, lambda b, i: (0,) * len(shape),
                                       pipeline_mode=pl.Buffered(1))
    wide = jax.ShapeDtypeStruct((bsz, seq, d), F32)
    narrow = jax.ShapeDtypeStruct((bsz, seq, d), BF16)
    return pl.pallas_call(
        _proj_kernel,
        grid=(bsz, seq // rows),
        in_specs=[tile, const((PERM_ROWS, PERM_ROWS)), const((1, d)), const(w_in.shape),
                  const((1, 2 * d)), const((CONV_WIDTH, d))],
        out_specs=[tile] * 7,
        out_shape=[wide] * 3 + [narrow] * 4,
        scratch_shapes=[pltpu.VMEM((2 * SUBLANES, d), F32)],
        compiler_params=pltpu.CompilerParams(
            dimension_semantics=("arbitrary", "arbitrary"),
            vmem_limit_bytes=VMEM_LIMIT_BYTES),
        name="proj_stage",
    )(x, _to_storage_order(PERM_ROWS), norm_g, w_in, b_merge, conv_w)


def _attn_kernel(bias_ref, q_ref, k_ref, v_ref, o_ref, stat_ref, kbuf, vbuf, *,
                 lead, chained):
    n_blocks, d = int(np.prod(lead)), q_ref.shape[-1]
    n_chains = q_ref.shape[-3] if chained else 1
    step = pl.program_id(2)
    first = (step == 0).astype(jnp.int32)

    @pl.when(step == 0)
    def _():
        for c in range(n_chains):
            kbuf[c * n_blocks] = jnp.zeros((QB, d), BF16)
            vbuf[c * n_blocks] = jnp.zeros((QB, d), BF16)

    lane = lax.broadcasted_iota(jnp.int32, (QB, LANES), 1)
    head_masks = [(lane // HEAD_DIM) == e for e in range(HEADS_PER_LANE_TILE)]
    rows = (slice(None),) * 2

    for c, g in [(c, g) for c in range(n_chains) for g in range(n_blocks)]:
        chain = (c,) if chained else ()
        where = tuple(int(i) for i in np.unravel_index(g, lead))
        slot = c * n_blocks + g
        keep = c * n_blocks + (g + 1) % n_blocks
        stat_tile = jnp.zeros((QB, LANES), F32)
        for t in range(N_HEADS // HEADS_PER_LANE_TILE):
            cols = slice(t * LANES, (t + 1) * LANES)
            at = where + rows + chain + (slice(None), cols)
            block = lambda ref: ref[at].reshape(QB, LANES).astype(BF16)
            q, k_cur, v_cur = block(q_ref), block(k_ref), block(v_ref)
            k = jnp.concatenate([kbuf[slot, :, cols], k_cur], axis=0)
            v = jnp.concatenate([vbuf[slot, :, cols], v_cur], axis=0)
            kbuf[keep, :, cols] = k_cur
            vbuf[keep, :, cols] = v_cur
            q_heads = jnp.concatenate(
                [jnp.where(m, q, jnp.zeros_like(q)) for m in head_masks], axis=0)
            s_all = lax.dot_general(q_heads, k, (((1,), (1,)), ((), ())),
                                    preferred_element_type=F32)
            ps = []
            for e in range(HEADS_PER_LANE_TILE):
                h = t * HEADS_PER_LANE_TILE + e
                s = s_all[e * QB:(e + 1) * QB, :] + bias_ref[first if g == 0 else 0, h]
                m = jnp.max(s, axis=-1, keepdims=True)
                p = jnp.exp2(s - m)
                stat_tile = jnp.where(lane == h, m, stat_tile)
                stat_tile = jnp.where(lane == N_HEADS + h,
                                      jnp.sum(p, axis=-1, keepdims=True), stat_tile)
                ps.append(p.astype(BF16))
            acc_all = jnp.dot(jnp.concatenate(ps, axis=0), v, preferred_element_type=F32)
            o_tile = acc_all[0:QB, :]
            for e in range(1, HEADS_PER_LANE_TILE):
                o_tile = jnp.where(head_masks[e], acc_all[e * QB:(e + 1) * QB, :], o_tile)
            o_ref[at] = o_tile.reshape(4, 4, SUBLANES, LANES)
        stat_at = where + rows + chain
        stat_ref[stat_at] = stat_tile.reshape(4, 4, SUBLANES, LANES)


def _block_positions(dilation):
    row = np.arange(QB)
    if dilation == 16:
        return row
    if dilation == 4:
        return (row // 32) * 32 + (row % 8) * 4 + (row // 8) % 4
    if dilation == 1:
        return (row % 8) * 16 + row // 8
    raise NotImplementedError(dilation)


def _alibi_bias(dilation):
    pos = _block_positions(dilation)
    q_loc = pos[:, None] + QB
    k_loc = np.concatenate([pos, pos + QB])[None, :]
    delta = q_loc - k_loc
    valid = (delta >= 0) & (delta <= QB)
    has_prev = np.stack([np.ones((2 * QB,), bool), np.arange(2 * QB) >= QB])
    ok = jnp.asarray(valid[None, None] & has_prev[:, None, None, :])
    slopes = jnp.exp2(-8.0 * jnp.arange(1, N_HEADS + 1, dtype=F32) / N_HEADS)
    bias = -slopes[:, None, None] * jnp.asarray(dilation * delta, F32)[None]
    return jnp.where(ok, bias[None] * LOG2_E, -jnp.inf)


def _box_spec(dilation, width, g, chains):
    if dilation == 16 and chains > 1:
        low = 4 // chains
        return pl.BlockSpec((None, g, 4, 4, None, chains, SUBLANES, width),
                            lambda b, r, i: (b, i, 0, 0, r // low, r % low, 0, 0))
    assert chains == 1
    if dilation == 16:
        return pl.BlockSpec((None, g, 4, 4, None, None, SUBLANES, width),
                            lambda b, r, i: (b, i, 0, 0, r // 4, r % 4, 0, 0))
    if g > 4:
        hi = g // 4
        if dilation == 4:
            return pl.BlockSpec((None, hi, 4, 4, 4, None, SUBLANES, width),
                                lambda b, r, i: (b, i, 0, 0, 0, r, 0, 0))
        if dilation == 1:
            per = 4 // hi
            return pl.BlockSpec((None, None, hi, 4, 4, 4, SUBLANES, width),
                                lambda b, r, i: (b, i // per, i % per, 0, 0, 0, 0, 0))
    per = 4 // g
    if dilation == 4:
        return pl.BlockSpec((None, None, g, 4, 4, None, SUBLANES, width),
                            lambda b, r, i: (b, i // per, i % per, 0, 0, r, 0, 0))
    if dilation == 1:
        return pl.BlockSpec((None, None, None, g, 4, 4, SUBLANES, width),
                            lambda b, r, i: (b, i // (4 * per), (i // per) % 4, i % per, 0, 0, 0, 0))
    raise NotImplementedError(dilation)


def _attn_stage(q, k, v, dilation):
    bsz, seq, d = q.shape
    nb = seq // (dilation * QB)
    view = lambda a: a.reshape(bsz, seq // 2048, 4, 4, 4, 4, SUBLANES, a.shape[-1])
    bias = _alibi_bias(dilation)
    bias_spec = pl.BlockSpec(bias.shape, lambda b, r, i: (0, 0, 0, 0),
                             pipeline_mode=pl.Buffered(1))
    g = min(ATTN_BLOCKS, nb)
    chains = min(ATTN_BLOCKS // g, 4) if dilation == 16 else 1
    lead = (g,) if g <= 4 or dilation == 16 else (g // 4, 4)
    box = _box_spec(dilation, d, g, chains)
    stat_box = _box_spec(dilation, LANES, g, chains)
    stat_shape = jax.ShapeDtypeStruct(view(q).shape[:-1] + (LANES,), F32)
    o, stat = pl.pallas_call(
        functools.partial(_attn_kernel, lead=lead, chained=chains > 1),
        grid=(bsz, dilation // chains, nb // g),
        in_specs=[bias_spec, box, box, box],
        out_specs=[box, stat_box],
        out_shape=[jax.ShapeDtypeStruct(view(q).shape, F32), stat_shape],
        scratch_shapes=[pltpu.VMEM((chains * g, QB, d), BF16)] * 2,
        compiler_params=pltpu.CompilerParams(
            dimension_semantics=("arbitrary", "arbitrary", "arbitrary"),
            vmem_limit_bytes=VMEM_LIMIT_BYTES),
        name=f"attn_stage_d{dilation}",
    )(bias, view(q), view(k), view(v))
    return o.reshape(bsz * seq, d), stat.reshape(bsz * seq, LANES)


def _out_kernel(*refs, n_patterns):
    x_ref, ac_ref, za_ref, gc_ref, ga_ref = refs[:5]
    o_refs = refs[5:5 + n_patterns]
    stat_refs = refs[5 + n_patterns:5 + 2 * n_patterns]
    spread_ref, perm_ref, woc_ref, woa_ref, wo_ref, fg_ref, y_ref = refs[5 + 2 * n_patterns:]

    ms = [r[...] for r in stat_refs]
    lane = lax.broadcasted_iota(jnp.int32, ms[0].shape, 1)
    ls = [jnp.where(lane < N_HEADS, pltpu.roll(m, LANES - N_HEADS, 1), 1.0) for m in ms]
    top = functools.reduce(jnp.maximum, ms)
    ws = [jnp.exp2(m - top) for m in ms]
    total = functools.reduce(jnp.add, [w * l for w, l in zip(ws, ls)])
    mixed_o = None
    for w, o_ref in zip(ws, o_refs):
        alpha = w / total
        hi = alpha.astype(BF16)
        lo = (alpha - hi.astype(F32)).astype(BF16)
        wide = jnp.dot(jnp.concatenate([hi, lo], axis=1), spread_ref[...],
                       preferred_element_type=F32)
        term = wide * o_ref[...]
        mixed_o = term if mixed_o is None else mixed_o + term
    gated = (za_ref[...].astype(F32) * mixed_o).astype(BF16)
    y_attn = jnp.dot(gated, woa_ref[...], preferred_element_type=F32)
    y_conv = jnp.dot(ac_ref[...], woc_ref[...], preferred_element_type=F32)
    merged = gc_ref[...].astype(F32) * y_conv + ga_ref[...].astype(F32) * y_attn
    merged = _permute_rows(perm_ref[...], merged.astype(BF16))
    mixed = jnp.dot(merged, wo_ref[...], preferred_element_type=F32)
    y_ref[...] = _rms_norm(x_ref[...] + mixed, fg_ref[...])


def _head_spread(d):
    s = np.zeros((2 * LANES, d), np.float32)
    for h in range(N_HEADS):
        s[h, h * HEAD_DIM:(h + 1) * HEAD_DIM] = 1.0
        s[LANES + h, h * HEAD_DIM:(h + 1) * HEAD_DIM] = 1.0
    return jnp.asarray(s, BF16)


def _out_stage(x, ac, za, gc, ga, os_, stats, w_out_conv, w_out_attn, w_o, final_g):
    n, d = x.shape
    rows = OUT_ROWS
    n_patterns = len(os_)
    tile = pl.BlockSpec((rows, d), lambda i: (i, 0))
    stat = pl.BlockSpec((rows, LANES), lambda i: (i, 0))
    const = lambda shape: pl.BlockSpec(shape, lambda i: (0,) * len(shape),
                                       pipeline_mode=pl.Buffered(1))
    return pl.pallas_call(
        functools.partial(_out_kernel, n_patterns=n_patterns),
        grid=(n // rows,),
        in_specs=([tile] * (5 + n_patterns) + [stat] * n_patterns
                  + [const((2 * LANES, d)), const((PERM_ROWS, PERM_ROWS))]
                  + [const((d, d))] * 3 + [const((1, d))]),
        out_specs=tile,
        out_shape=jax.ShapeDtypeStruct((n, d), F32),
        compiler_params=pltpu.CompilerParams(
            dimension_semantics=("arbitrary",),
            vmem_limit_bytes=VMEM_LIMIT_BYTES),
        name="out_stage",
    )(x, ac, za, gc, ga, *os_, *stats, _head_spread(d), _to_storage_order(PERM_ROWS).T,
      w_out_conv, w_out_attn, w_o, final_g)


def _layer(h, norm_g, w_in, b_merge, conv_w, w_out_conv, w_out_attn, w_o, out_g):
    bsz, seq, d = h.shape
    q, k, v, za, ac, gc, ga = _proj_stage(
        h, norm_g[None, :], w_in.astype(BF16), b_merge[None, :], conv_w)
    os_, stats = zip(*[_attn_stage(q, k, v, dilation) for _, dilation in ATTN_PATTERNS])
    flat = lambda a: a.reshape(bsz * seq, d)
    y = _out_stage(flat(h), flat(ac), flat(za), flat(gc), flat(ga), os_, stats,
                   w_out_conv.astype(BF16), w_out_attn.astype(BF16), w_o.astype(BF16),
                   out_g[None, :])
    return y.reshape(bsz, seq, d)


def kernel(x, norm_g, w_in, b_merge, conv_w, w_out_conv, w_out_attn, w_o, final_g):
    depth = norm_g.shape[0]
    assert depth == 1, "the fused output stage applies the final norm after a single layer"
    assert x.shape[-1] == N_HEADS * HEAD_DIM and x.shape[1] % 4096 == 0
    assert ATTN_PATTERNS == ((128, 1), (512, 4), (2048, 16)), "box specs are per pattern"
    return _layer(x, norm_g[0], w_in[0], b_merge[0], conv_w[0],
                  w_out_conv[0], w_out_attn[0], w_o[0], final_g)
```

```python
import functools

import jax
import jax.numpy as jnp
import numpy as np
from jax import lax
from jax.experimental import pallas as pl
from jax.experimental.pallas import tpu as pltpu

N_HEADS = 16
HEAD_DIM = 64
CONV_WIDTH = 3
ATTN_PATTERNS = ((128, 1), (512, 4), (2048, 16))
QB = 128
EPS = 1e-6
LOG2_E = 1.4426950408889634

LANES = 128
SUBLANES = 8
HEADS_PER_LANE_TILE = LANES // HEAD_DIM
GROUPS = QB // SUBLANES
ATTN_BLOCKS = 8
PERM_ROWS = 256
PROJ_ROWS = 512
OUT_ROWS = 512
VMEM_LIMIT_BYTES = 56 * 1024 * 1024

BF16 = jnp.bfloat16
F32 = jnp.float32


def _rms_norm(x, g):
    return x * lax.rsqrt(jnp.mean(x * x, axis=-1, keepdims=True) + EPS) * g


def _to_storage_order(rows):
    s = np.arange(rows)
    natural = (s // QB) * QB + (s % SUBLANES) * GROUPS + (s % QB) // SUBLANES
    p = np.zeros((rows, rows), np.float32)
    p[s, natural] = 1.0
    return jnp.asarray(p, BF16)


def _permute_rows(p, a):
    n = p.shape[0]
    return jnp.concatenate(
        [jnp.dot(p, a[i:i + n, :], preferred_element_type=F32).astype(BF16)
         for i in range(0, a.shape[0], n)], axis=0)


def _previous_token(a, carry_group):
    rows, d = a.shape
    first_row = lax.broadcasted_iota(jnp.int32, (SUBLANES, d), 0) == 0
    pieces = []
    prev_last = carry_group
    for blk in range(rows // QB):
        base = blk * QB
        last = a[base + QB - SUBLANES:base + QB, :]
        pieces.append(jnp.where(first_row, pltpu.roll(prev_last, 1, 0), pltpu.roll(last, 1, 0)))
        pieces.append(a[base:base + QB - SUBLANES, :])
        prev_last = last
    return jnp.concatenate(pieces, axis=0)


def _proj_kernel(x_ref, perm_ref, g_ref, w_ref, b_ref, cw_ref,
                 q_ref, k_ref, v_ref, za_ref, ac_ref, gc_ref, ga_ref, carry):
    rows, d = x_ref.shape
    u = _permute_rows(perm_ref[...], _rms_norm(x_ref[...], g_ref[...]).astype(BF16))

    def proj(c):
        return jnp.dot(u, w_ref[:, c * d:(c + 1) * d], preferred_element_type=F32)

    @pl.when(pl.program_id(1) == 0)
    def _():
        carry[...] = jnp.zeros(carry.shape, F32)

    a = proj(2) * proj(0)
    back1 = _previous_token(a, carry[SUBLANES:2 * SUBLANES, :])
    back2 = _previous_token(back1, carry[0:SUBLANES, :])
    carry[...] = a[rows - 2 * SUBLANES:rows, :]
    c = cw_ref[2:3, :] * a + cw_ref[1:2, :] * back1 + cw_ref[0:1, :] * back2
    ac_ref[...] = (jax.nn.silu(proj(3)) * proj(1) * c).astype(BF16)

    q_ref[...] = proj(4) * (HEAD_DIM ** -0.5 * LOG2_E)
    k_ref[...] = proj(5)
    v_ref[...] = proj(6)
    za_ref[...] = jax.nn.silu(proj(7)).astype(BF16)
    gc_ref[...] = jax.nn.sigmoid(proj(8) + b_ref[:, 0:d]).astype(BF16)
    ga_ref[...] = jax.nn.sigmoid(proj(9) + b_ref[:, d:2 * d]).astype(BF16)


def _proj_stage(x, norm_g, w_in, b_merge, conv_w):
    bsz, seq, d = x.shape
    rows = PROJ_ROWS
    tile = pl.BlockSpec((None, rows, d), lambda b, i: (b, i, 0))
    const = lambda shape: pl.BlockSpec(shape, lambda b, i: (0,) * len(shape),
                                       pipeline_mode=pl.Buffered(1))
    wide = jax.ShapeDtypeStruct((bsz, seq, d), F32)
    narrow = jax.ShapeDtypeStruct((bsz, seq, d), BF16)
    return pl.pallas_call(
        _proj_kernel,
        grid=(bsz, seq // rows),
        in_specs=[tile, const((PERM_ROWS, PERM_ROWS)), const((1, d)), const(w_in.shape),
                  const((1, 2 * d)), const((CONV_WIDTH, d))],
        out_specs=[tile] * 7,
        out_shape=[wide] * 3 + [narrow] * 4,
        scratch_shapes=[pltpu.VMEM((2 * SUBLANES, d), F32)],
        compiler_params=pltpu.CompilerParams(
            dimension_semantics=("arbitrary", "arbitrary"),
            vmem_limit_bytes=VMEM_LIMIT_BYTES),
        name="proj_stage",
    )(x, _to_storage_order(PERM_ROWS), norm_g, w_in, b_merge, conv_w)


def _attn_kernel(bias_ref, q_ref, k_ref, v_ref, o_ref, stat_ref, kbuf, vbuf, *,
                 lead, chained):
    n_blocks, d = int(np.prod(lead)), q_ref.shape[-1]
    n_chains = q_ref.shape[-3] if chained else 1
    step = pl.program_id(2)
    first = (step == 0).astype(jnp.int32)

    @pl.when(step == 0)
    def _():
        for c in range(n_chains):
            kbuf[c * n_blocks] = jnp.zeros((QB, d), BF16)
            vbuf[c * n_blocks] = jnp.zeros((QB, d), BF16)

    lane = lax.broadcasted_iota(jnp.int32, (QB, LANES), 1)
    head_masks = [(lane // HEAD_DIM) == e for e in range(HEADS_PER_LANE_TILE)]
    rows = (slice(None),) * 2

    for c, g in [(c, g) for c in range(n_chains) for g in range(n_blocks)]:
        chain = (c,) if chained else ()
        where = tuple(int(i) for i in np.unravel_index(g, lead))
        slot = c * n_blocks + g
        keep = c * n_blocks + (g + 1) % n_blocks
        stat_tile = jnp.zeros((QB, LANES), F32)
        for t in range(N_HEADS // HEADS_PER_LANE_TILE):
            cols = slice(t * LANES, (t + 1) * LANES)
            at = where + rows + chain + (slice(None), cols)
            block = lambda ref: ref[at].reshape(QB, LANES).astype(BF16)
            q, k_cur, v_cur = block(q_ref), block(k_ref), block(v_ref)
            k = jnp.concatenate([kbuf[slot, :, cols], k_cur], axis=0)
            v = jnp.concatenate([vbuf[slot, :, cols], v_cur], axis=0)
            kbuf[keep, :, cols] = k_cur
            vbuf[keep, :, cols] = v_cur
            q_heads = jnp.concatenate(
                [jnp.where(m, q, jnp.zeros_like(q)) for m in head_masks], axis=0)
            s_all = lax.dot_general(q_heads, k, (((1,), (1,)), ((), ())),
                                    preferred_element_type=F32)
            ps = []
            for e in range(HEADS_PER_LANE_TILE):
                h = t * HEADS_PER_LANE_TILE + e
                s = s_all[e * QB:(e + 1) * QB, :] + bias_ref[first if g == 0 else 0, h]
                m = jnp.max(s, axis=-1, keepdims=True)
                p = jnp.exp2(s - m)
                stat_tile = jnp.where(lane == h, m, stat_tile)
                stat_tile = jnp.where(lane == N_HEADS + h,
                                      jnp.sum(p, axis=-1, keepdims=True), stat_tile)
                ps.append(p.astype(BF16))
            acc_all = jnp.dot(jnp.concatenate(ps, axis=0), v, preferred_element_type=F32)
            o_tile = acc_all[0:QB, :]
            for e in range(1, HEADS_PER_LANE_TILE):
                o_tile = jnp.where(head_masks[e], acc_all[e * QB:(e + 1) * QB, :], o_tile)
            o_ref[at] = o_tile.reshape(4, 4, SUBLANES, LANES)
        stat_at = where + rows + chain
        stat_ref[stat_at] = stat_tile.reshape(4, 4, SUBLANES, LANES)


def _block_positions(dilation):
    row = np.arange(QB)
    if dilation == 16:
        return row
    if dilation == 4:
        return (row // 32) * 32 + (row % 8) * 4 + (row // 8) % 4
    if dilation == 1:
        return (row % 8) * 16 + row // 8
    raise NotImplementedError(dilation)


def _alibi_bias(dilation):
    pos = _block_positions(dilation)
    q_loc = pos[:, None] + QB
    k_loc = np.concatenate([pos, pos + QB])[None, :]
    delta = q_loc - k_loc
    valid = (delta >= 0) & (delta <= QB)
    has_prev = np.stack([np.ones((2 * QB,), bool), np.arange(2 * QB) >= QB])
    ok = jnp.asarray(valid[None, None] & has_prev[:, None, None, :])
    slopes = jnp.exp2(-8.0 * jnp.arange(1, N_HEADS + 1, dtype=F32) / N_HEADS)
    bias = -slopes[:, None, None] * jnp.asarray(dilation * delta, F32)[None]
    return jnp.where(ok, bias[None] * LOG2_E, -jnp.inf)


def _box_spec(dilation, width, g, chains):
    if dilation == 16 and chains > 1:
        low = 4 // chains
        return pl.BlockSpec((None, g, 4, 4, None, chains, SUBLANES, width),
                            lambda b, r, i: (b, i, 0, 0, r // low, r % low, 0, 0))
    assert chains == 1
    if dilation == 16:
        return pl.BlockSpec((None, g, 4, 4, None, None, SUBLANES, width),
                            lambda b, r, i: (b, i, 0, 0, r // 4, r % 4, 0, 0))
    if g > 4:
        hi = g // 4
        if dilation == 4:
            return pl.BlockSpec((None, hi, 4, 4, 4, None, SUBLANES, width),
                                lambda b, r, i: (b, i, 0, 0, 0, r, 0, 0))
        if dilation == 1:
            per = 4 // hi
            return pl.BlockSpec((None, None, hi, 4, 4, 4, SUBLANES, width),
                                lambda b, r, i: (b, i // per, i % per, 0, 0, 0, 0, 0))
    per = 4 // g
    if dilation == 4:
        return pl.BlockSpec((None, None, g, 4, 4, None, SUBLANES, width),
                            lambda b, r, i: (b, i // per, i % per, 0, 0, r, 0, 0))
    if dilation == 1:
        return pl.BlockSpec((None, None, None, g, 4, 4, SUBLANES, width),
                            lambda b, r, i: (b, i // (4 * per), (i // per) % 4, i % per, 0, 0, 0, 0))
    raise NotImplementedError(dilation)


def _attn_stage(q, k, v, dilation):
    bsz, seq, d = q.shape
    nb = seq // (dilation * QB)
    view = lambda a: a.reshape(bsz, seq // 2048, 4, 4, 4, 4, SUBLANES, a.shape[-1])
    bias = _alibi_bias(dilation)
    bias_spec = pl.BlockSpec(bias.shape, lambda b, r, i: (0, 0, 0, 0),
                             pipeline_mode=pl.Buffered(1))
    g = min(ATTN_BLOCKS, nb)
    chains = min(ATTN_BLOCKS // g, 4) if dilation == 16 else 1
    lead = (g,) if g <= 4 or dilation == 16 else (g // 4, 4)
    box = _box_spec(dilation, d, g, chains)
    stat_box = _box_spec(dilation, LANES, g, chains)
    stat_shape = jax.ShapeDtypeStruct(view(q).shape[:-1] + (LANES,), F32)
    o, stat = pl.pallas_call(
        functools.partial(_attn_kernel, lead=lead, chained=chains > 1),
        grid=(bsz, dilation // chains, nb // g),
        in_specs=[bias_spec, box, box, box],
        out_specs=[box, stat_box],
        out_shape=[jax.ShapeDtypeStruct(view(q).shape, F32), stat_shape],
        scratch_shapes=[pltpu.VMEM((chains * g, QB, d), BF16)] * 2,
        compiler_params=pltpu.CompilerParams(
            dimension_semantics=("arbitrary", "arbitrary", "arbitrary"),
            vmem_limit_bytes=VMEM_LIMIT_BYTES),
        name=f"attn_stage_d{dilation}",
    )(bias, view(q), view(k), view(v))
    return o.reshape(bsz * seq, d), stat.reshape(bsz * seq, LANES)


def _out_kernel(*refs, n_patterns):
    x_ref, ac_ref, za_ref, gc_ref, ga_ref = refs[:5]
    o_refs = refs[5:5 + n_patterns]
    stat_refs = refs[5 + n_patterns:5 + 2 * n_patterns]
    spread_ref, perm_ref, woc_ref, woa_ref, wo_ref, fg_ref, y_ref = refs[5 + 2 * n_patterns:]

    ms = [r[...] for r in stat_refs]
    lane = lax.broadcasted_iota(jnp.int32, ms[0].shape, 1)
    ls = [jnp.where(lane < N_HEADS, pltpu.roll(m, LANES - N_HEADS, 1), 1.0) for m in ms]
    top = functools.reduce(jnp.maximum, ms)
    ws = [jnp.exp2(m - top) for m in ms]
    total = functools.reduce(jnp.add, [w * l for w, l in zip(ws, ls)])
    mixed_o = None
    for w, o_ref in zip(ws, o_refs):
        alpha = w / total
        hi = alpha.astype(BF16)
        lo = (alpha - hi.astype(F32)).astype(BF16)
        wide = jnp.dot(jnp.concatenate([hi, lo], axis=1), spread_ref[...],
                       preferred_element_type=F32)
        term = wide * o_ref[...]
        mixed_o = term if mixed_o is None else mixed_o + term
    gated = (za_ref[...].astype(F32) * mixed_o).astype(BF16)
    y_attn = jnp.dot(gated, woa_ref[...], preferred_element_type=F32)
    y_conv = jnp.dot(ac_ref[...], woc_ref[...], preferred_element_type=F32)
    merged = gc_ref[...].astype(F32) * y_conv + ga_ref[...].astype(F32) * y_attn
    merged = _permute_rows(perm_ref[...], merged.astype(BF16))
    mixed = jnp.dot(merged, wo_ref[...], preferred_element_type=F32)
    y_ref[...] = _rms_norm(x_ref[...] + mixed, fg_ref[...])


def _head_spread(d):
    s = np.zeros((2 * LANES, d), np.float32)
    for h in range(N_HEADS):
        s[h, h * HEAD_DIM:(h + 1) * HEAD_DIM] = 1.0
        s[LANES + h, h * HEAD_DIM:(h + 1) * HEAD_DIM] = 1.0
    return jnp.asarray(s, BF16)


def _out_stage(x, ac, za, gc, ga, os_, stats, w_out_conv, w_out_attn, w_o, final_g):
    n, d = x.shape
    rows = OUT_ROWS
    n_patterns = len(os_)
    tile = pl.BlockSpec((rows, d), lambda i: (i, 0))
    stat = pl.BlockSpec((rows, LANES), lambda i: (i, 0))
    const = lambda shape: pl.BlockSpec(shape, lambda i: (0,) * len(shape),
                                       pipeline_mode=pl.Buffered(1))
    return pl.pallas_call(
        functools.partial(_out_kernel, n_patterns=n_patterns),
        grid=(n // rows,),
        in_specs=([tile] * (5 + n_patterns) + [stat] * n_patterns
                  + [const((2 * LANES, d)), const((PERM_ROWS, PERM_ROWS))]
                  + [const((d, d))] * 3 + [const((1, d))]),
        out_specs=tile,
        out_shape=jax.ShapeDtypeStruct((n, d), F32),
        compiler_params=pltpu.CompilerParams(
            dimension_semantics=("arbitrary",),
            vmem_limit_bytes=VMEM_LIMIT_BYTES),
        name="out_stage",
    )(x, ac, za, gc, ga, *os_, *stats, _head_spread(d), _to_storage_order(PERM_ROWS).T,
      w_out_conv, w_out_attn, w_o, final_g)


def _layer(h, norm_g, w_in, b_merge, conv_w, w_out_conv, w_out_attn, w_o, out_g):
    bsz, seq, d = h.shape
    q, k, v, za, ac, gc, ga = _proj_stage(
        h, norm_g[None, :], w_in.astype(BF16), b_merge[None, :], conv_w)
    os_, stats = zip(*[_attn_stage(q, k, v, dilation) for _, dilation in ATTN_PATTERNS])
    flat = lambda a: a.reshape(bsz * seq, d)
    y = _out_stage(flat(h), flat(ac), flat(za), flat(gc), flat(ga), os_, stats,
                   w_out_conv.astype(BF16), w_out_attn.astype(BF16), w_o.astype(BF16),
                   out_g[None, :])
    return y.reshape(bsz, seq, d)


def kernel(x, norm_g, w_in, b_merge, conv_w, w_out_conv, w_out_attn, w_o, final_g):
    depth = norm_g.shape[0]
    assert depth == 1, "the fused output stage applies the final norm after a single layer"
    assert x.shape[-1] == N_HEADS * HEAD_DIM and x.shape[1] % 4096 == 0
    assert ATTN_PATTERNS == ((128, 1), (512, 4), (2048, 16)), "box specs are per pattern"
    return _layer(x, norm_g[0], w_in[0], b_merge[0], conv_w[0],
                  w_out_conv[0], w_out_attn[0], w_o[0], final_g)
```

```python
import functools

import jax
import jax.numpy as jnp
import numpy as np
from jax import lax
from jax.experimental import pallas as pl
from jax.experimental.pallas import tpu as pltpu

N_HEADS = 16
HEAD_DIM = 64
CONV_WIDTH = 3
ATTN_PATTERNS = ((128, 1), (512, 4), (2048, 16))
QB = 128
EPS = 1e-6
LOG2_E = 1.4426950408889634

LANES = 128
SUBLANES = 8
HEADS_PER_LANE_TILE = LANES // HEAD_DIM
GROUPS = QB // SUBLANES
ATTN_BLOCKS = 8
PERM_ROWS = 256
PROJ_ROWS = 256
OUT_ROWS = 512
VMEM_LIMIT_BYTES = 56 * 1024 * 1024

BF16 = jnp.bfloat16
F32 = jnp.float32


def _rms_norm(x, g):
    return x * lax.rsqrt(jnp.mean(x * x, axis=-1, keepdims=True) + EPS) * g


def _to_storage_order(rows):
    s = np.arange(rows)
    natural = (s // QB) * QB + (s % SUBLANES) * GROUPS + (s % QB) // SUBLANES
    p = np.zeros((rows, rows), np.float32)
    p[s, natural] = 1.0
    return jnp.asarray(p, BF16)


def _permute_rows(p, a):
    n = p.shape[0]
    return jnp.concatenate(
        [jnp.dot(p, a[i:i + n, :], preferred_element_type=F32).astype(BF16)
         for i in range(0, a.shape[0], n)], axis=0)


def _previous_token(a, carry_group):
    rows, d = a.shape
    first_row = lax.broadcasted_iota(jnp.int32, (SUBLANES, d), 0) == 0
    pieces = []
    prev_last = carry_group
    for blk in range(rows // QB):
        base = blk * QB
        last = a[base + QB - SUBLANES:base + QB, :]
        pieces.append(jnp.where(first_row, pltpu.roll(prev_last, 1, 0), pltpu.roll(last, 1, 0)))
        pieces.append(a[base:base + QB - SUBLANES, :])
        prev_last = last
    return jnp.concatenate(pieces, axis=0)


def _proj_kernel(x_ref, perm_ref, g_ref, w_ref, b_ref, cw_ref,
                 q_ref, k_ref, v_ref, za_ref, ac_ref, gc_ref, ga_ref, carry):
    rows, d = x_ref.shape

    @pl.when(pl.program_id(1) == 0)
    def _():
        carry[...] = jnp.zeros(carry.shape, F32)

    u = _permute_rows(perm_ref[...], _rms_norm(x_ref[...], g_ref[...]).astype(BF16))

    def proj(c):
        return jnp.dot(u, w_ref[:, c * d:(c + 1) * d], preferred_element_type=F32)

    a = proj(2) * proj(0)
    back1 = _previous_token(a, carry[SUBLANES:2 * SUBLANES, :])
    back2 = _previous_token(back1, carry[0:SUBLANES, :])
    carry[...] = a[rows - 2 * SUBLANES:rows, :]
    c = cw_ref[2:3, :] * a + cw_ref[1:2, :] * back1 + cw_ref[0:1, :] * back2
    ac_ref[...] = (jax.nn.silu(proj(3)) * proj(1) * c).astype(BF16)

    za_ref[...] = jax.nn.silu(proj(7)).astype(BF16)
    gc_ref[...] = jax.nn.sigmoid(proj(8) + b_ref[:, 0:d]).astype(BF16)
    ga_ref[...] = jax.nn.sigmoid(proj(9) + b_ref[:, d:2 * d]).astype(BF16)
    q_ref[...] = proj(4) * (HEAD_DIM ** -0.5 * LOG2_E)
    k_ref[...] = proj(5)
    v_ref[...] = proj(6)


def _proj_stage(x, norm_g, w_in, b_merge, conv_w):
    bsz, seq, d = x.shape
    rows = PROJ_ROWS
    tile = pl.BlockSpec((None, rows, d), lambda b, i: (b, i, 0))
    const = lambda shape: pl.BlockSpec(shape, lambda b, i: (0,) * len(shape),
                                       pipeline_mode=pl.Buffered(1))
    wide = jax.ShapeDtypeStruct((bsz, seq, d), F32)
    narrow = jax.ShapeDtypeStruct((bsz, seq, d), BF16)
    return pl.pallas_call(
        _proj_kernel,
        grid=(bsz, seq // rows),
        in_specs=[tile, const((PERM_ROWS, PERM_ROWS)), const((1, d)), const(w_in.shape),
                  const((1, 2 * d)), const((CONV_WIDTH, d))],
        out_specs=[tile] * 7,
        out_shape=[wide] * 3 + [narrow] * 4,
        scratch_shapes=[pltpu.VMEM((2 * SUBLANES, d), F32)],
        compiler_params=pltpu.CompilerParams(
            dimension_semantics=("arbitrary", "arbitrary"),
            vmem_limit_bytes=VMEM_LIMIT_BYTES),
        name="proj_stage",
    )(x, _to_storage_order(PERM_ROWS), norm_g, w_in, b_merge, conv_w)


def _attn_kernel(bias_ref, q_ref, k_ref, v_ref, o_ref, stat_ref, kbuf, vbuf, *,
                 lead, chained):
    n_blocks, d = int(np.prod(lead)), q_ref.shape[-1]
    n_chains = q_ref.shape[-3] if chained else 1
    step = pl.program_id(2)
    first = (step == 0).astype(jnp.int32)

    @pl.when(step == 0)
    def _():
        for c in range(n_chains):
            kbuf[c * n_blocks] = jnp.zeros((QB, d), BF16)
            vbuf[c * n_blocks] = jnp.zeros((QB, d), BF16)

    lane = lax.broadcasted_iota(jnp.int32, (QB, LANES), 1)
    head_masks = [(lane // HEAD_DIM) == e for e in range(HEADS_PER_LANE_TILE)]
    rows = (slice(None),) * 2

    for c, g in [(c, g) for c in range(n_chains) for g in range(n_blocks)]:
        chain = (c,) if chained else ()
        where = tuple(int(i) for i in np.unravel_index(g, lead))
        slot = c * n_blocks + g
        keep = c * n_blocks + (g + 1) % n_blocks
        stat_tile = jnp.zeros((QB, LANES), F32)
        for t in range(N_HEADS // HEADS_PER_LANE_TILE):
            cols = slice(t * LANES, (t + 1) * LANES)
            at = where + rows + chain + (slice(None), cols)
            block = lambda ref: ref[at].reshape(QB, LANES).astype(BF16)
            q, k_cur, v_cur = block(q_ref), block(k_ref), block(v_ref)
            k = jnp.concatenate([kbuf[slot, :, cols], k_cur], axis=0)
            v = jnp.concatenate([vbuf[slot, :, cols], v_cur], axis=0)
            kbuf[keep, :, cols] = k_cur
            vbuf[keep, :, cols] = v_cur
            q_heads = jnp.concatenate(
                [jnp.where(m, q, jnp.zeros_like(q)) for m in head_masks], axis=0)
            s_all = lax.dot_general(q_heads, k, (((1,), (1,)), ((), ())),
                                    preferred_element_type=F32)
            ps = []
            for e in range(HEADS_PER_LANE_TILE):
                h = t * HEADS_PER_LANE_TILE + e
                s = s_all[e * QB:(e + 1) * QB, :] + bias_ref[first if g == 0 else 0, h]
                m = jnp.max(s, axis=-1, keepdims=True)
                p = jnp.exp2(s - m)
                stat_tile = jnp.where(lane == h, m, stat_tile)
                stat_tile = jnp.where(lane == N_HEADS + h,
                                      jnp.sum(p, axis=-1, keepdims=True), stat_tile)
                ps.append(p.astype(BF16))
            acc_all = jnp.dot(jnp.concatenate(ps, axis=0), v, preferred_element_type=F32)
            o_tile = acc_all[0:QB, :]
            for e in range(1, HEADS_PER_LANE_TILE):
                o_tile = jnp.where(head_masks[e], acc_all[e * QB:(e + 1) * QB, :], o_tile)
            o_ref[at] = o_tile.reshape(4, 4, SUBLANES, LANES)
        stat_at = where + rows + chain
        stat_ref[stat_at] = stat_tile.reshape(4, 4, SUBLANES, LANES)


def _block_positions(dilation):
    row = np.arange(QB)
    if dilation == 16:
        return row
    if dilation == 4:
        return (row // 32) * 32 + (row % 8) * 4 + (row // 8) % 4
    if dilation == 1:
        return (row % 8) * 16 + row // 8
    raise NotImplementedError(dilation)


def _alibi_bias(dilation):
    pos = _block_positions(dilation)
    q_loc = pos[:, None] + QB
    k_loc = np.concatenate([pos, pos + QB])[None, :]
    delta = q_loc - k_loc
    valid = (delta >= 0) & (delta <= QB)
    has_prev = np.stack([np.ones((2 * QB,), bool), np.arange(2 * QB) >= QB])
    ok = jnp.asarray(valid[None, None] & has_prev[:, None, None, :])
    slopes = jnp.exp2(-8.0 * jnp.arange(1, N_HEADS + 1, dtype=F32) / N_HEADS)
    bias = -slopes[:, None, None] * jnp.asarray(dilation * delta, F32)[None]
    return jnp.where(ok, bias[None] * LOG2_E, -jnp.inf)


def _box_spec(dilation, width, g, chains):
    if dilation == 16 and chains > 1:
        low = 4 // chains
        return pl.BlockSpec((None, g, 4, 4, None, chains, SUBLANES, width),
                            lambda b, r, i: (b, i, 0, 0, r // low, r % low, 0, 0))
    assert chains == 1
    if dilation == 16:
        return pl.BlockSpec((None, g, 4, 4, None, None, SUBLANES, width),
                            lambda b, r, i: (b, i, 0, 0, r // 4, r % 4, 0, 0))
    if g > 4:
        hi = g // 4
        if dilation == 4:
            return pl.BlockSpec((None, hi, 4, 4, 4, None, SUBLANES, width),
                                lambda b, r, i: (b, i, 0, 0, 0, r, 0, 0))
        if dilation == 1:
            per = 4 // hi
            return pl.BlockSpec((None, None, hi, 4, 4, 4, SUBLANES, width),
                                lambda b, r, i: (b, i // per, i % per, 0, 0, 0, 0, 0))
    per = 4 // g
    if dilation == 4:
        return pl.BlockSpec((None, None, g, 4, 4, None, SUBLANES, width),
                            lambda b, r, i: (b, i // per, i % per, 0, 0, r, 0, 0))
    if dilation == 1:
        return pl.BlockSpec((None, None, None, g, 4, 4, SUBLANES, width),
                            lambda b, r, i: (b, i // (4 * per), (i // per) % 4, i % per, 0, 0, 0, 0))
    raise NotImplementedError(dilation)


def _attn_stage(q, k, v, dilation):
    bsz, seq, d = q.shape
    nb = seq // (dilation * QB)
    view = lambda a: a.reshape(bsz, seq // 2048, 4, 4, 4, 4, SUBLANES, a.shape[-1])
    bias = _alibi_bias(dilation)
    bias_spec = pl.BlockSpec(bias.shape, lambda b, r, i: (0, 0, 0, 0),
                             pipeline_mode=pl.Buffered(1))
    g = min(ATTN_BLOCKS, nb)
    chains = min(ATTN_BLOCKS // g, 4) if dilation == 16 else 1
    lead = (g,) if g <= 4 or dilation == 16 else (g // 4, 4)
    box = _box_spec(dilation, d, g, chains)
    stat_box = _box_spec(dilation, LANES, g, chains)
    stat_shape = jax.ShapeDtypeStruct(view(q).shape[:-1] + (LANES,), F32)
    o, stat = pl.pallas_call(
        functools.partial(_attn_kernel, lead=lead, chained=chains > 1),
        grid=(bsz, dilation // chains, nb // g),
        in_specs=[bias_spec, box, box, box],
        out_specs=[box, stat_box],
        out_shape=[jax.ShapeDtypeStruct(view(q).shape, F32), stat_shape],
        scratch_shapes=[pltpu.VMEM((chains * g, QB, d), BF16)] * 2,
        compiler_params=pltpu.CompilerParams(
            dimension_semantics=("arbitrary", "arbitrary", "arbitrary"),
            vmem_limit_bytes=VMEM_LIMIT_BYTES),
        name=f"attn_stage_d{dilation}",
    )(bias, view(q), view(k), view(v))
    return o.reshape(bsz * seq, d), stat.reshape(bsz * seq, LANES)


def _out_kernel(*refs, n_patterns):
    x_ref, ac_ref, za_ref, gc_ref, ga_ref = refs[:5]
    o_refs = refs[5:5 + n_patterns]
    stat_refs = refs[5 + n_patterns:5 + 2 * n_patterns]
    spread_ref, perm_ref, woc_ref, woa_ref, wo_ref, fg_ref, y_ref = refs[5 + 2 * n_patterns:]

    ms = [r[...] for r in stat_refs]
    lane = lax.broadcasted_iota(jnp.int32, ms[0].shape, 1)
    ls = [jnp.where(lane < N_HEADS, pltpu.roll(m, LANES - N_HEADS, 1), 1.0) for m in ms]
    top = functools.reduce(jnp.maximum, ms)
    ws = [jnp.exp2(m - top) for m in ms]
    total = functools.reduce(jnp.add, [w * l for w, l in zip(ws, ls)])
    mixed_o = None
    for w, o_ref in zip(ws, o_refs):
        alpha = w / total
        hi = alpha.astype(BF16)
        lo = (alpha - hi.astype(F32)).astype(BF16)
        wide = jnp.dot(jnp.concatenate([hi, lo], axis=1), spread_ref[...],
                       preferred_element_type=F32)
        term = wide * o_ref[...]
        mixed_o = term if mixed_o is None else mixed_o + term
    gated = (za_ref[...].astype(F32) * mixed_o).astype(BF16)
    y_attn = jnp.dot(gated, woa_ref[...], preferred_element_type=F32)
    y_conv = jnp.dot(ac_ref[...], woc_ref[...], preferred_element_type=F32)
    merged = gc_ref[...].astype(F32) * y_conv + ga_ref[...].astype(F32) * y_attn
    merged = _permute_rows(perm_ref[...], merged.astype(BF16))
    mixed = jnp.dot(merged, wo_ref[...], preferred_element_type=F32)
    y_ref[...] = _rms_norm(x_ref[...] + mixed, fg_ref[...])


def _head_spread(d):
    s = np.zeros((2 * LANES, d), np.float32)
    for h in range(N_HEADS):
        s[h, h * HEAD_DIM:(h + 1) * HEAD_DIM] = 1.0
        s[LANES + h, h * HEAD_DIM:(h + 1) * HEAD_DIM] = 1.0
    return jnp.asarray(s, BF16)


def _out_stage(x, ac, za, gc, ga, os_, stats, w_out_conv, w_out_attn, w_o, final_g):
    n, d = x.shape
    rows = OUT_ROWS
    n_patterns = len(os_)
    tile = pl.BlockSpec((rows, d), lambda i: (i, 0))
    stat = pl.BlockSpec((rows, LANES), lambda i: (i, 0))
    const = lambda shape: pl.BlockSpec(shape, lambda i: (0,) * len(shape),
                                       pipeline_mode=pl.Buffered(1))
    return pl.pallas_call(
        functools.partial(_out_kernel, n_patterns=n_patterns),
        grid=(n // rows,),
        in_specs=([tile] * (5 + n_patterns) + [stat] * n_patterns
                  + [const((2 * LANES, d)), const((PERM_ROWS, PERM_ROWS))]
                  + [const((d, d))] * 3 + [const((1, d))]),
        out_specs=tile,
        out_shape=jax.ShapeDtypeStruct((n, d), F32),
        compiler_params=pltpu.CompilerParams(
            dimension_semantics=("arbitrary",),
            vmem_limit_bytes=VMEM_LIMIT_BYTES),
        name="out_stage",
    )(x, ac, za, gc, ga, *os_, *stats, _head_spread(d), _to_storage_order(PERM_ROWS).T,
      w_out_conv, w_out_attn, w_o, final_g)


def _layer(h, norm_g, w_in, b_merge, conv_w, w_out_conv, w_out_attn, w_o, out_g):
    bsz, seq, d = h.shape
    q, k, v, za, ac, gc, ga = _proj_stage(
        h, norm_g[None, :], w_in.astype(BF16), b_merge[None, :], conv_w)
    os_, stats = zip(*[_attn_stage(q, k, v, dilation) for _, dilation in ATTN_PATTERNS])
    flat = lambda a: a.reshape(bsz * seq, d)
    y = _out_stage(flat(h), flat(ac), flat(za), flat(gc), flat(ga), os_, stats,
                   w_out_conv.astype(BF16), w_out_attn.astype(BF16), w_o.astype(BF16),
                   out_g[None, :])
    return y.reshape(bsz, seq, d)


def kernel(x, norm_g, w_in, b_merge, conv_w, w_out_conv, w_out_attn, w_o, final_g):
    depth = norm_g.shape[0]
    assert depth == 1, "the fused output stage applies the final norm after a single layer"
    assert x.shape[-1] == N_HEADS * HEAD_DIM and x.shape[1] % 4096 == 0
    assert ATTN_PATTERNS == ((128, 1), (512, 4), (2048, 16)), "box specs are per pattern"
    return _layer(x, norm_g[0], w_in[0], b_merge[0], conv_w[0],
                  w_out_conv[0], w_out_attn[0], w_o[0], final_g)
```

```python
import functools

import jax
import jax.numpy as jnp
import numpy as np
from jax import lax
from jax.experimental import pallas as pl
from jax.experimental.pallas import tpu as pltpu

N_HEADS = 16
HEAD_DIM = 64
CONV_WIDTH = 3
ATTN_PATTERNS = ((128, 1), (512, 4), (2048, 16))
QB = 128
EPS = 1e-6
LOG2_E = 1.4426950408889634

LANES = 128
SUBLANES = 8
HEADS_PER_LANE_TILE = LANES // HEAD_DIM
GROUPS = QB // SUBLANES
ATTN_BLOCKS = 8
PERM_ROWS = 256
PROJ_ROWS = 256
OUT_ROWS = 512
VMEM_LIMIT_BYTES = 56 * 1024 * 1024

BF16 = jnp.bfloat16
F32 = jnp.float32


def _rms_norm(x, g):
    return x * lax.rsqrt(jnp.mean(x * x, axis=-1, keepdims=True) + EPS) * g


def _to_storage_order(rows):
    s = np.arange(rows)
    natural = (s // QB) * QB + (s % SUBLANES) * GROUPS + (s % QB) // SUBLANES
    p = np.zeros((rows, rows), np.float32)
    p[s, natural] = 1.0
    return jnp.asarray(p, BF16)


def _permute_rows(p, a):
    n = p.shape[0]
    return jnp.concatenate(
        [jnp.dot(p, a[i:i + n, :], preferred_element_type=F32).astype(BF16)
         for i in range(0, a.shape[0], n)], axis=0)


def _previous_token(a, carry_group):
    rows, d = a.shape
    first_row = lax.broadcasted_iota(jnp.int32, (SUBLANES, d), 0) == 0
    pieces = []
    prev_last = carry_group
    for blk in range(rows // QB):
        base = blk * QB
        last = a[base + QB - SUBLANES:base + QB, :]
        pieces.append(jnp.where(first_row, pltpu.roll(prev_last, 1, 0), pltpu.roll(last, 1, 0)))
        pieces.append(a[base:base + QB - SUBLANES, :])
        prev_last = last
    return jnp.concatenate(pieces, axis=0)


def _proj_kernel(x_ref, perm_ref, g_ref, w_ref, b_ref, cw_ref,
                 q_ref, k_ref, v_ref, za_ref, ac_ref, gc_ref, ga_ref, carry):
    rows, d = x_ref.shape

    @pl.when(pl.program_id(1) == 0)
    def _():
        carry[...] = jnp.zeros(carry.shape, F32)

    for r in range(0, rows, PERM_ROWS):
        chunk = slice(r, r + PERM_ROWS)
        u = _permute_rows(perm_ref[...],
                          _rms_norm(x_ref[chunk, :], g_ref[...]).astype(BF16))

        def proj(c, u=u):
            return jnp.dot(u, w_ref[:, c * d:(c + 1) * d], preferred_element_type=F32)

        a = proj(2) * proj(0)
        back1 = _previous_token(a, carry[SUBLANES:2 * SUBLANES, :])
        back2 = _previous_token(back1, carry[0:SUBLANES, :])
        carry[...] = a[PERM_ROWS - 2 * SUBLANES:PERM_ROWS, :]
        c = cw_ref[2:3, :] * a + cw_ref[1:2, :] * back1 + cw_ref[0:1, :] * back2
        ac_ref[chunk, :] = (jax.nn.silu(proj(3)) * proj(1) * c).astype(BF16)

        za_ref[chunk, :] = jax.nn.silu(proj(7)).astype(BF16)
        gc_ref[chunk, :] = jax.nn.sigmoid(proj(8) + b_ref[:, 0:d]).astype(BF16)
        ga_ref[chunk, :] = jax.nn.sigmoid(proj(9) + b_ref[:, d:2 * d]).astype(BF16)
        q_ref[chunk, :] = proj(4) * (HEAD_DIM ** -0.5 * LOG2_E)
        k_ref[chunk, :] = proj(5)
        v_ref[chunk, :] = proj(6)


def _proj_stage(x, norm_g, w_in, b_merge, conv_w):
    bsz, seq, d = x.shape
    rows = PROJ_ROWS
    tile = pl.BlockSpec((None, rows, d), lambda b, i: (b, i, 0))
    const = lambda shape: pl.BlockSpec(shape, lambda b, i: (0,) * len(shape),
                                       pipeline_mode=pl.Buffered(1))
    wide = jax.ShapeDtypeStruct((bsz, seq, d), F32)
    narrow = jax.ShapeDtypeStruct((bsz, seq, d), BF16)
    return pl.pallas_call(
        _proj_kernel,
        grid=(bsz, seq // rows),
        in_specs=[tile, const((PERM_ROWS, PERM_ROWS)), const((1, d)), const(w_in.shape),
                  const((1, 2 * d)), const((CONV_WIDTH, d))],
        out_specs=[tile] * 7,
        out_shape=[wide] * 3 + [narrow] * 4,
        scratch_shapes=[pltpu.VMEM((2 * SUBLANES, d), F32)],
        compiler_params=pltpu.CompilerParams(
            dimension_semantics=("arbitrary", "arbitrary"),
            vmem_limit_bytes=VMEM_LIMIT_BYTES),
        name="proj_stage",
    )(x, _to_storage_order(PERM_ROWS), norm_g, w_in, b_merge, conv_w)


def _attn_kernel(bias_ref, q_ref, k_ref, v_ref, o_ref, stat_ref, kbuf, vbuf, *,
                 lead, chained):
    n_blocks, d = int(np.prod(lead)), q_ref.shape[-1]
    n_chains = q_ref.shape[-3] if chained else 1
    step = pl.program_id(2)
    first = (step == 0).astype(jnp.int32)

    @pl.when(step == 0)
    def _():
        for c in range(n_chains):
            kbuf[c * n_blocks] = jnp.zeros((QB, d), BF16)
            vbuf[c * n_blocks] = jnp.zeros((QB, d), BF16)

    lane = lax.broadcasted_iota(jnp.int32, (QB, LANES), 1)
    head_masks = [(lane // HEAD_DIM) == e for e in range(HEADS_PER_LANE_TILE)]
    rows = (slice(None),) * 2

    for c, g in [(c, g) for c in range(n_chains) for g in range(n_blocks)]:
        chain = (c,) if chained else ()
        where = tuple(int(i) for i in np.unravel_index(g, lead))
        slot = c * n_blocks + g
        keep = c * n_blocks + (g + 1) % n_blocks
        stat_tile = jnp.zeros((QB, LANES), F32)
        for t in range(N_HEADS // HEADS_PER_LANE_TILE):
            cols = slice(t * LANES, (t + 1) * LANES)
            at = where + rows + chain + (slice(None), cols)
            block = lambda ref: ref[at].reshape(QB, LANES).astype(BF16)
            q, k_cur, v_cur = block(q_ref), block(k_ref), block(v_ref)
            k = jnp.concatenate([kbuf[slot, :, cols], k_cur], axis=0)
            v = jnp.concatenate([vbuf[slot, :, cols], v_cur], axis=0)
            kbuf[keep, :, cols] = k_cur
            vbuf[keep, :, cols] = v_cur
            q_heads = jnp.concatenate(
                [jnp.where(m, q, jnp.zeros_like(q)) for m in head_masks], axis=0)
            s_all = lax.dot_general(q_heads, k, (((1,), (1,)), ((), ())),
                                    preferred_element_type=F32)
            ps = []
            for e in range(HEADS_PER_LANE_TILE):
                h = t * HEADS_PER_LANE_TILE + e
                s = s_all[e * QB:(e + 1) * QB, :] + bias_ref[first if g == 0 else 0, h]
                m = jnp.max(s, axis=-1, keepdims=True)
                p = jnp.exp2(s - m)
                stat_tile = jnp.where(lane == h, m, stat_tile)
                stat_tile = jnp.where(lane == N_HEADS + h,
                                      jnp.sum(p, axis=-1, keepdims=True), stat_tile)
                ps.append(p.astype(BF16))
            acc_all = jnp.dot(jnp.concatenate(ps, axis=0), v, preferred_element_type=F32)
            o_tile = acc_all[0:QB, :]
            for e in range(1, HEADS_PER_LANE_TILE):
                o_tile = jnp.where(head_masks[e], acc_all[e * QB:(e + 1) * QB, :], o_tile)
            o_ref[at] = o_tile.reshape(4, 4, SUBLANES, LANES)
        stat_at = where + rows + chain
        stat_ref[stat_at] = stat_tile.reshape(4, 4, SUBLANES, LANES)


def _block_positions(dilation):
    row = np.arange(QB)
    if dilation == 16:
        return row
    if dilation == 4:
        return (row // 32) * 32 + (row % 8) * 4 + (row // 8) % 4
    if dilation == 1:
        return (row % 8) * 16 + row // 8
    raise NotImplementedError(dilation)


def _alibi_bias(dilation):
    pos = _block_positions(dilation)
    q_loc = pos[:, None] + QB
    k_loc = np.concatenate([pos, pos + QB])[None, :]
    delta = q_loc - k_loc
    valid = (delta >= 0) & (delta <= QB)
    has_prev = np.stack([np.ones((2 * QB,), bool), np.arange(2 * QB) >= QB])
    ok = valid[None, None] & has_prev[:, None, None, :]
    slopes = np.exp2(-8.0 * np.arange(1, N_HEADS + 1) / N_HEADS)
    bias = -slopes[:, None, None] * (dilation * delta)[None] * LOG2_E
    return jnp.asarray(np.where(ok, bias[None], -np.inf), F32)


def _box_spec(dilation, width, g, chains):
    if dilation == 16 and chains > 1:
        low = 4 // chains
        return pl.BlockSpec((None, g, 4, 4, None, chains, SUBLANES, width),
                            lambda b, r, i: (b, i, 0, 0, r // low, r % low, 0, 0))
    assert chains == 1
    if dilation == 16:
        return pl.BlockSpec((None, g, 4, 4, None, None, SUBLANES, width),
                            lambda b, r, i: (b, i, 0, 0, r // 4, r % 4, 0, 0))
    if g > 4:
        hi = g // 4
        if dilation == 4:
            return pl.BlockSpec((None, hi, 4, 4, 4, None, SUBLANES, width),
                                lambda b, r, i: (b, i, 0, 0, 0, r, 0, 0))
        if dilation == 1:
            per = 4 // hi
            return pl.BlockSpec((None, None, hi, 4, 4, 4, SUBLANES, width),
                                lambda b, r, i: (b, i // per, i % per, 0, 0, 0, 0, 0))
    per = 4 // g
    if dilation == 4:
        return pl.BlockSpec((None, None, g, 4, 4, None, SUBLANES, width),
                            lambda b, r, i: (b, i // per, i % per, 0, 0, r, 0, 0))
    if dilation == 1:
        return pl.BlockSpec((None, None, None, g, 4, 4, SUBLANES, width),
                            lambda b, r, i: (b, i // (4 * per), (i // per) % 4, i % per, 0, 0, 0, 0))
    raise NotImplementedError(dilation)


def _attn_stage(q, k, v, dilation):
    bsz, seq, d = q.shape
    nb = seq // (dilation * QB)
    view = lambda a: a.reshape(bsz, seq // 2048, 4, 4, 4, 4, SUBLANES, a.shape[-1])
    bias = _alibi_bias(dilation)
    bias_spec = pl.BlockSpec(bias.shape, lambda b, r, i: (0, 0, 0, 0),
                             pipeline_mode=pl.Buffered(1))
    g = min(ATTN_BLOCKS, nb)
    chains = min(ATTN_BLOCKS // g, 4) if dilation == 16 else 1
    lead = (g,) if g <= 4 or dilation == 16 else (g // 4, 4)
    box = _box_spec(dilation, d, g, chains)
    stat_box = _box_spec(dilation, LANES, g, chains)
    stat_shape = jax.ShapeDtypeStruct(view(q).shape[:-1] + (LANES,), F32)
    o, stat = pl.pallas_call(
        functools.partial(_attn_kernel, lead=lead, chained=chains > 1),
        grid=(bsz, dilation // chains, nb // g),
        in_specs=[bias_spec, box, box, box],
        out_specs=[box, stat_box],
        out_shape=[jax.ShapeDtypeStruct(view(q).shape, F32), stat_shape],
        scratch_shapes=[pltpu.VMEM((chains * g, QB, d), BF16)] * 2,
        compiler_params=pltpu.CompilerParams(
            dimension_semantics=("arbitrary", "arbitrary", "arbitrary"),
            vmem_limit_bytes=VMEM_LIMIT_BYTES),
        name=f"attn_stage_d{dilation}",
    )(bias, view(q), view(k), view(v))
    return o.reshape(bsz * seq, d), stat.reshape(bsz * seq, LANES)


def _out_kernel(*refs, n_patterns):
    x_ref, ac_ref, za_ref, gc_ref, ga_ref = refs[:5]
    o_refs = refs[5:5 + n_patterns]
    stat_refs = refs[5 + n_patterns:5 + 2 * n_patterns]
    spread_ref, perm_ref, woc_ref, woa_ref, wo_ref, fg_ref, y_ref = refs[5 + 2 * n_patterns:]

    ms = [r[...] for r in stat_refs]
    lane = lax.broadcasted_iota(jnp.int32, ms[0].shape, 1)
    ls = [jnp.where(lane < N_HEADS, pltpu.roll(m, LANES - N_HEADS, 1), 1.0) for m in ms]
    top = functools.reduce(jnp.maximum, ms)
    ws = [jnp.exp2(m - top) for m in ms]
    total = functools.reduce(jnp.add, [w * l for w, l in zip(ws, ls)])
    mixed_o = None
    for w, o_ref in zip(ws, o_refs):
        alpha = w / total
        hi = alpha.astype(BF16)
        lo = (alpha - hi.astype(F32)).astype(BF16)
        wide = jnp.dot(jnp.concatenate([hi, lo], axis=1), spread_ref[...],
                       preferred_element_type=F32)
        term = wide * o_ref[...]
        mixed_o = term if mixed_o is None else mixed_o + term
    gated = (za_ref[...].astype(F32) * mixed_o).astype(BF16)
    y_attn = jnp.dot(gated, woa_ref[...], preferred_element_type=F32)
    y_conv = jnp.dot(ac_ref[...], woc_ref[...], preferred_element_type=F32)
    merged = gc_ref[...].astype(F32) * y_conv + ga_ref[...].astype(F32) * y_attn
    merged = _permute_rows(perm_ref[...], merged.astype(BF16))
    for r in range(0, merged.shape[0], PERM_ROWS):
        chunk = slice(r, r + PERM_ROWS)
        mixed = jnp.dot(merged[chunk, :], wo_ref[...], preferred_element_type=F32)
        y_ref[chunk, :] = _rms_norm(x_ref[chunk, :] + mixed, fg_ref[...])


def _head_spread(d):
    s = np.zeros((2 * LANES, d), np.float32)
    for h in range(N_HEADS):
        s[h, h * HEAD_DIM:(h + 1) * HEAD_DIM] = 1.0
        s[LANES + h, h * HEAD_DIM:(h + 1) * HEAD_DIM] = 1.0
    return jnp.asarray(s, BF16)


def _out_stage(x, ac, za, gc, ga, os_, stats, w_out_conv, w_out_attn, w_o, final_g):
    n, d = x.shape
    rows = OUT_ROWS
    n_patterns = len(os_)
    tile = pl.BlockSpec((rows, d), lambda i: (i, 0))
    stat = pl.BlockSpec((rows, LANES), lambda i: (i, 0))
    const = lambda shape: pl.BlockSpec(shape, lambda i: (0,) * len(shape),
                                       pipeline_mode=pl.Buffered(1))
    return pl.pallas_call(
        functools.partial(_out_kernel, n_patterns=n_patterns),
        grid=(n // rows,),
        in_specs=([tile] * (5 + n_patterns) + [stat] * n_patterns
                  + [const((2 * LANES, d)), const((PERM_ROWS, PERM_ROWS))]
                  + [const((d, d))] * 3 + [const((1, d))]),
        out_specs=tile,
        out_shape=jax.ShapeDtypeStruct((n, d), F32),
        compiler_params=pltpu.CompilerParams(
            dimension_semantics=("arbitrary",),
            vmem_limit_bytes=VMEM_LIMIT_BYTES),
        name="out_stage",
    )(x, ac, za, gc, ga, *os_, *stats, _head_spread(d), _to_storage_order(PERM_ROWS).T,
      w_out_conv, w_out_attn, w_o, final_g)


def _layer(h, norm_g, w_in, b_merge, conv_w, w_out_conv, w_out_attn, w_o, out_g):
    bsz, seq, d = h.shape
    q, k, v, za, ac, gc, ga = _proj_stage(
        h, norm_g[None, :], w_in.astype(BF16), b_merge[None, :], conv_w)
    os_, stats = zip(*[_attn_stage(q, k, v, dilation) for _, dilation in ATTN_PATTERNS])
    flat = lambda a: a.reshape(bsz * seq, d)
    y = _out_stage(flat(h), flat(ac), flat(za), flat(gc), flat(ga), os_, stats,
                   w_out_conv.astype(BF16), w_out_attn.astype(BF16), w_o.astype(BF16),
                   out_g[None, :])
    return y.reshape(bsz, seq, d)


def kernel(x, norm_g, w_in, b_merge, conv_w, w_out_conv, w_out_attn, w_o, final_g):
    depth = norm_g.shape[0]
    assert depth == 1, "the fused output stage applies the final norm after a single layer"
    assert x.shape[-1] == N_HEADS * HEAD_DIM and x.shape[1] % 4096 == 0
    assert ATTN_PATTERNS == ((128, 1), (512, 4), (2048, 16)), "box specs are per pattern"
    return _layer(x, norm_g[0], w_in[0], b_merge[0], conv_w[0],
                  w_out_conv[0], w_out_attn[0], w_o[0], final_g)
```

```python
import functools

import jax
import jax.numpy as jnp
import numpy as np
from jax import lax
from jax.experimental import pallas as pl
from jax.experimental.pallas import tpu as pltpu

N_HEADS = 16
HEAD_DIM = 64
CONV_WIDTH = 3
ATTN_PATTERNS = ((128, 1), (512, 4), (2048, 16))
QB = 128
EPS = 1e-6
LOG2_E = 1.4426950408889634

LANES = 128
SUBLANES = 8
HEADS_PER_LANE_TILE = LANES // HEAD_DIM
GROUPS = QB // SUBLANES
ATTN_BLOCKS = 8
PERM_ROWS = 256
PROJ_ROWS = 256
OUT_ROWS = 512
VMEM_LIMIT_BYTES = 56 * 1024 * 1024

BF16 = jnp.bfloat16
F32 = jnp.float32


def _rms_norm(x, g):
    return x * lax.rsqrt(jnp.mean(x * x, axis=-1, keepdims=True) + EPS) * g


def _to_storage_order(rows):
    s = np.arange(rows)
    natural = (s // QB) * QB + (s % SUBLANES) * GROUPS + (s % QB) // SUBLANES
    p = np.zeros((rows, rows), np.float32)
    p[s, natural] = 1.0
    return jnp.asarray(p, BF16)


def _permute_rows(p, a):
    n = p.shape[0]
    return jnp.concatenate(
        [jnp.dot(p, a[i:i + n, :], preferred_element_type=F32).astype(BF16)
         for i in range(0, a.shape[0], n)], axis=0)


def _previous_token(a, carry_group):
    rows, d = a.shape
    first_row = lax.broadcasted_iota(jnp.int32, (SUBLANES, d), 0) == 0
    pieces = []
    prev_last = carry_group
    for blk in range(rows // QB):
        base = blk * QB
        last = a[base + QB - SUBLANES:base + QB, :]
        pieces.append(jnp.where(first_row, pltpu.roll(prev_last, 1, 0), pltpu.roll(last, 1, 0)))
        pieces.append(a[base:base + QB - SUBLANES, :])
        prev_last = last
    return jnp.concatenate(pieces, axis=0)


def _proj_kernel(x_ref, perm_ref, g_ref, w_ref, b_ref, cw_ref,
                 q_ref, k_ref, v_ref, za_ref, ac_ref, gc_ref, ga_ref, carry):
    rows, d = x_ref.shape

    @pl.when(pl.program_id(1) == 0)
    def _():
        carry[...] = jnp.zeros(carry.shape, F32)

    for r in range(0, rows, PERM_ROWS):
        chunk = slice(r, r + PERM_ROWS)
        u = _permute_rows(perm_ref[...],
                          _rms_norm(x_ref[chunk, :], g_ref[...]).astype(BF16))

        def proj(c, u=u):
            return jnp.dot(u, w_ref[:, c * d:(c + 1) * d], preferred_element_type=F32)

        a = proj(2) * proj(0)
        back1 = _previous_token(a, carry[SUBLANES:2 * SUBLANES, :])
        back2 = _previous_token(back1, carry[0:SUBLANES, :])
        carry[...] = a[PERM_ROWS - 2 * SUBLANES:PERM_ROWS, :]
        c = cw_ref[2:3, :] * a + cw_ref[1:2, :] * back1 + cw_ref[0:1, :] * back2
        ac_ref[chunk, :] = (jax.nn.silu(proj(3)) * proj(1) * c).astype(BF16)

        za_ref[chunk, :] = jax.nn.silu(proj(7)).astype(BF16)
        gc_ref[chunk, :] = jax.nn.sigmoid(proj(8) + b_ref[:, 0:d]).astype(BF16)
        ga_ref[chunk, :] = jax.nn.sigmoid(proj(9) + b_ref[:, d:2 * d]).astype(BF16)
        q_ref[chunk, :] = proj(4) * (HEAD_DIM ** -0.5 * LOG2_E)
        k_ref[chunk, :] = proj(5)
        v_ref[chunk, :] = proj(6)


def _proj_stage(x, norm_g, w_in, b_merge, conv_w):
    bsz, seq, d = x.shape
    rows = PROJ_ROWS
    tile = pl.BlockSpec((None, rows, d), lambda b, i: (b, i, 0))
    const = lambda shape: pl.BlockSpec(shape, lambda b, i: (0,) * len(shape),
                                       pipeline_mode=pl.Buffered(1))
    wide = jax.ShapeDtypeStruct((bsz, seq, d), F32)
    narrow = jax.ShapeDtypeStruct((bsz, seq, d), BF16)
    return pl.pallas_call(
        _proj_kernel,
        grid=(bsz, seq // rows),
        in_specs=[tile, const((PERM_ROWS, PERM_ROWS)), const((1, d)), const(w_in.shape),
                  const((1, 2 * d)), const((CONV_WIDTH, d))],
        out_specs=[tile] * 7,
        out_shape=[wide] * 3 + [narrow] * 4,
        scratch_shapes=[pltpu.VMEM((2 * SUBLANES, d), F32)],
        compiler_params=pltpu.CompilerParams(
            dimension_semantics=("arbitrary", "arbitrary"),
            vmem_limit_bytes=VMEM_LIMIT_BYTES),
        name="proj_stage",
    )(x, _to_storage_order(PERM_ROWS), norm_g, w_in, b_merge, conv_w)


def _attn_kernel(bias_ref, q_ref, k_ref, v_ref, o_ref, stat_ref, kbuf, vbuf, *,
                 lead, chained):
    n_blocks, d = int(np.prod(lead)), q_ref.shape[-1]
    n_chains = q_ref.shape[-3] if chained else 1
    step = pl.program_id(2)
    first = (step == 0).astype(jnp.int32)

    @pl.when(step == 0)
    def _():
        for c in range(n_chains):
            kbuf[c * n_blocks] = jnp.zeros((QB, d), BF16)
            vbuf[c * n_blocks] = jnp.zeros((QB, d), BF16)

    lane = lax.broadcasted_iota(jnp.int32, (QB, LANES), 1)
    head_masks = [(lane // HEAD_DIM) == e for e in range(HEADS_PER_LANE_TILE)]
    rows = (slice(None),) * 2

    for c, g in [(c, g) for c in range(n_chains) for g in range(n_blocks)]:
        chain = (c,) if chained else ()
        where = tuple(int(i) for i in np.unravel_index(g, lead))
        slot = c * n_blocks + g
        keep = c * n_blocks + (g + 1) % n_blocks
        stat_tile = jnp.zeros((QB, LANES), F32)
        for t in range(N_HEADS // HEADS_PER_LANE_TILE):
            cols = slice(t * LANES, (t + 1) * LANES)
            at = where + rows + chain + (slice(None), cols)
            block = lambda ref: ref[at].reshape(QB, LANES).astype(BF16)
            q, k_cur, v_cur = block(q_ref), block(k_ref), block(v_ref)
            k = jnp.concatenate([kbuf[slot, :, cols], k_cur], axis=0)
            v = jnp.concatenate([vbuf[slot, :, cols], v_cur], axis=0)
            kbuf[keep, :, cols] = k_cur
            vbuf[keep, :, cols] = v_cur
            q_heads = jnp.concatenate(
                [jnp.where(m, q, jnp.zeros_like(q)) for m in head_masks], axis=0)
            s_all = lax.dot_general(q_heads, k, (((1,), (1,)), ((), ())),
                                    preferred_element_type=F32)
            ps = []
            for e in range(HEADS_PER_LANE_TILE):
                h = t * HEADS_PER_LANE_TILE + e
                s = s_all[e * QB:(e + 1) * QB, :] + bias_ref[first if g == 0 else 0, h]
                m = jnp.max(s, axis=-1, keepdims=True)
                p = jnp.exp2(s - m)
                stat_tile = jnp.where(lane == h, m, stat_tile)
                ps.append(p.astype(BF16))
            acc_all = jnp.dot(jnp.concatenate(ps, axis=0),
                              jnp.concatenate([v, jnp.ones_like(v)], axis=1),
                              preferred_element_type=F32)
            o_tile = acc_all[0:QB, 0:LANES]
            for e in range(HEADS_PER_LANE_TILE):
                h = t * HEADS_PER_LANE_TILE + e
                mine = acc_all[e * QB:(e + 1) * QB, :]
                stat_tile = jnp.where(lane == N_HEADS + h, mine[:, LANES:], stat_tile)
                if e:
                    o_tile = jnp.where(head_masks[e], mine[:, 0:LANES], o_tile)
            o_ref[at] = o_tile.reshape(4, 4, SUBLANES, LANES)
        stat_at = where + rows + chain
        stat_ref[stat_at] = stat_tile.reshape(4, 4, SUBLANES, LANES)


def _block_positions(dilation):
    row = np.arange(QB)
    if dilation == 16:
        return row
    if dilation == 4:
        return (row // 32) * 32 + (row % 8) * 4 + (row // 8) % 4
    if dilation == 1:
        return (row % 8) * 16 + row // 8
    raise NotImplementedError(dilation)


def _alibi_bias(dilation):
    pos = _block_positions(dilation)
    q_loc = pos[:, None] + QB
    k_loc = np.concatenate([pos, pos + QB])[None, :]
    delta = q_loc - k_loc
    valid = (delta >= 0) & (delta <= QB)
    has_prev = np.stack([np.ones((2 * QB,), bool), np.arange(2 * QB) >= QB])
    ok = valid[None, None] & has_prev[:, None, None, :]
    slopes = np.exp2(-8.0 * np.arange(1, N_HEADS + 1) / N_HEADS)
    bias = -slopes[:, None, None] * (dilation * delta)[None] * LOG2_E
    return jnp.asarray(np.where(ok, bias[None], -np.inf), F32)


def _box_spec(dilation, width, g, chains):
    if dilation == 16 and chains > 1:
        low = 4 // chains
        return pl.BlockSpec((None, g, 4, 4, None, chains, SUBLANES, width),
                            lambda b, r, i: (b, i, 0, 0, r // low, r % low, 0, 0))
    assert chains == 1
    if dilation == 16:
        return pl.BlockSpec((None, g, 4, 4, None, None, SUBLANES, width),
                            lambda b, r, i: (b, i, 0, 0, r // 4, r % 4, 0, 0))
    if g > 4:
        hi = g // 4
        if dilation == 4:
            return pl.BlockSpec((None, hi, 4, 4, 4, None, SUBLANES, width),
                                lambda b, r, i: (b, i, 0, 0, 0, r, 0, 0))
        if dilation == 1:
            per = 4 // hi
            return pl.BlockSpec((None, None, hi, 4, 4, 4, SUBLANES, width),
                                lambda b, r, i: (b, i // per, i % per, 0, 0, 0, 0, 0))
    per = 4 // g
    if dilation == 4:
        return pl.BlockSpec((None, None, g, 4, 4, None, SUBLANES, width),
                            lambda b, r, i: (b, i // per, i % per, 0, 0, r, 0, 0))
    if dilation == 1:
        return pl.BlockSpec((None, None, None, g, 4, 4, SUBLANES, width),
                            lambda b, r, i: (b, i // (4 * per), (i // per) % 4, i % per, 0, 0, 0, 0))
    raise NotImplementedError(dilation)


def _attn_stage(q, k, v, dilation):
    bsz, seq, d = q.shape
    nb = seq // (dilation * QB)
    view = lambda a: a.reshape(bsz, seq // 2048, 4, 4, 4, 4, SUBLANES, a.shape[-1])
    bias = _alibi_bias(dilation)
    bias_spec = pl.BlockSpec(bias.shape, lambda b, r, i: (0, 0, 0, 0),
                             pipeline_mode=pl.Buffered(1))
    g = min(ATTN_BLOCKS, nb)
    chains = min(ATTN_BLOCKS // g, 4) if dilation == 16 else 1
    lead = (g,) if g <= 4 or dilation == 16 else (g // 4, 4)
    box = _box_spec(dilation, d, g, chains)
    stat_box = _box_spec(dilation, LANES, g, chains)
    stat_shape = jax.ShapeDtypeStruct(view(q).shape[:-1] + (LANES,), F32)
    o, stat = pl.pallas_call(
        functools.partial(_attn_kernel, lead=lead, chained=chains > 1),
        grid=(bsz, dilation // chains, nb // g),
        in_specs=[bias_spec, box, box, box],
        out_specs=[box, stat_box],
        out_shape=[jax.ShapeDtypeStruct(view(q).shape, F32), stat_shape],
        scratch_shapes=[pltpu.VMEM((chains * g, QB, d), BF16)] * 2,
        compiler_params=pltpu.CompilerParams(
            dimension_semantics=("arbitrary", "arbitrary", "arbitrary"),
            vmem_limit_bytes=VMEM_LIMIT_BYTES),
        name=f"attn_stage_d{dilation}",
    )(bias, view(q), view(k), view(v))
    return o.reshape(bsz * seq, d), stat.reshape(bsz * seq, LANES)


def _out_kernel(*refs, n_patterns):
    x_ref, ac_ref, za_ref, gc_ref, ga_ref = refs[:5]
    o_refs = refs[5:5 + n_patterns]
    stat_refs = refs[5 + n_patterns:5 + 2 * n_patterns]
    spread_ref, perm_ref, woc_ref, woa_ref, wo_ref, fg_ref, y_ref = refs[5 + 2 * n_patterns:]

    ms = [r[...] for r in stat_refs]
    lane = lax.broadcasted_iota(jnp.int32, ms[0].shape, 1)
    ls = [jnp.where(lane < N_HEADS, pltpu.roll(m, LANES - N_HEADS, 1), 1.0) for m in ms]
    top = functools.reduce(jnp.maximum, ms)
    ws = [jnp.exp2(m - top) for m in ms]
    total = functools.reduce(jnp.add, [w * l for w, l in zip(ws, ls)])
    mixed_o = None
    for w, o_ref in zip(ws, o_refs):
        alpha = w / total
        hi = alpha.astype(BF16)
        lo = (alpha - hi.astype(F32)).astype(BF16)
        wide = jnp.dot(jnp.concatenate([hi, lo], axis=1), spread_ref[...],
                       preferred_element_type=F32)
        term = wide * o_ref[...]
        mixed_o = term if mixed_o is None else mixed_o + term
    gated = (za_ref[...].astype(F32) * mixed_o).astype(BF16)
    y_attn = jnp.dot(gated, woa_ref[...], preferred_element_type=F32)
    y_conv = jnp.dot(ac_ref[...], woc_ref[...], preferred_element_type=F32)
    merged = gc_ref[...].astype(F32) * y_conv + ga_ref[...].astype(F32) * y_attn
    merged = _permute_rows(perm_ref[...], merged.astype(BF16))
    for r in range(0, merged.shape[0], PERM_ROWS):
        chunk = slice(r, r + PERM_ROWS)
        mixed = jnp.dot(merged[chunk, :], wo_ref[...], preferred_element_type=F32)
        y_ref[chunk, :] = _rms_norm(x_ref[chunk, :] + mixed, fg_ref[...])


def _head_spread(d):
    s = np.zeros((2 * LANES, d), np.float32)
    for h in range(N_HEADS):
        s[h, h * HEAD_DIM:(h + 1) * HEAD_DIM] = 1.0
        s[LANES + h, h * HEAD_DIM:(h + 1) * HEAD_DIM] = 1.0
    return jnp.asarray(s, BF16)


def _out_stage(x, ac, za, gc, ga, os_, stats, w_out_conv, w_out_attn, w_o, final_g):
    n, d = x.shape
    rows = OUT_ROWS
    n_patterns = len(os_)
    tile = pl.BlockSpec((rows, d), lambda i: (i, 0))
    stat = pl.BlockSpec((rows, LANES), lambda i: (i, 0))
    const = lambda shape: pl.BlockSpec(shape, lambda i: (0,) * len(shape),
                                       pipeline_mode=pl.Buffered(1))
    return pl.pallas_call(
        functools.partial(_out_kernel, n_patterns=n_patterns),
        grid=(n // rows,),
        in_specs=([tile] * (5 + n_patterns) + [stat] * n_patterns
                  + [const((2 * LANES, d)), const((PERM_ROWS, PERM_ROWS))]
                  + [const((d, d))] * 3 + [const((1, d))]),
        out_specs=tile,
        out_shape=jax.ShapeDtypeStruct((n, d), F32),
        compiler_params=pltpu.CompilerParams(
            dimension_semantics=("arbitrary",),
            vmem_limit_bytes=VMEM_LIMIT_BYTES),
        name="out_stage",
    )(x, ac, za, gc, ga, *os_, *stats, _head_spread(d), _to_storage_order(PERM_ROWS).T,
      w_out_conv, w_out_attn, w_o, final_g)


def _layer(h, norm_g, w_in, b_merge, conv_w, w_out_conv, w_out_attn, w_o, out_g):
    bsz, seq, d = h.shape
    q, k, v, za, ac, gc, ga = _proj_stage(
        h, norm_g[None, :], w_in.astype(BF16), b_merge[None, :], conv_w)
    os_, stats = zip(*[_attn_stage(q, k, v, dilation) for _, dilation in ATTN_PATTERNS])
    flat = lambda a: a.reshape(bsz * seq, d)
    y = _out_stage(flat(h), flat(ac), flat(za), flat(gc), flat(ga), os_, stats,
                   w_out_conv.astype(BF16), w_out_attn.astype(BF16), w_o.astype(BF16),
                   out_g[None, :])
    return y.reshape(bsz, seq, d)


def kernel(x, norm_g, w_in, b_merge, conv_w, w_out_conv, w_out_attn, w_o, final_g):
    depth = norm_g.shape[0]
    assert depth == 1, "the fused output stage applies the final norm after a single layer"
    assert x.shape[-1] == N_HEADS * HEAD_DIM and x.shape[1] % 4096 == 0
    assert ATTN_PATTERNS == ((128, 1), (512, 4), (2048, 16)), "box specs are per pattern"
    return _layer(x, norm_g[0], w_in[0], b_merge[0], conv_w[0],
                  w_out_conv[0], w_out_attn[0], w_o[0], final_g)
```

```python
import functools

import jax
import jax.numpy as jnp
import numpy as np
from jax import lax
from jax.experimental import pallas as pl
from jax.experimental.pallas import tpu as pltpu

N_HEADS = 16
HEAD_DIM = 64
CONV_WIDTH = 3
ATTN_PATTERNS = ((128, 1), (512, 4), (2048, 16))
QB = 128
EPS = 1e-6
LOG2_E = 1.4426950408889634

LANES = 128
SUBLANES = 8
HEADS_PER_LANE_TILE = LANES // HEAD_DIM
GROUPS = QB // SUBLANES
ATTN_BLOCKS = 8
PERM_ROWS = 256
PROJ_ROWS = 256
OUT_ROWS = 512
VMEM_LIMIT_BYTES = 56 * 1024 * 1024

BF16 = jnp.bfloat16
F32 = jnp.float32


def _rms_norm(x, g):
    return x * lax.rsqrt(jnp.mean(x * x, axis=-1, keepdims=True) + EPS) * g


def _to_storage_order(rows):
    s = np.arange(rows)
    natural = (s // QB) * QB + (s % SUBLANES) * GROUPS + (s % QB) // SUBLANES
    p = np.zeros((rows, rows), np.float32)
    p[s, natural] = 1.0
    return jnp.asarray(p, BF16)


def _permute_rows(p, a):
    n = p.shape[0]
    return jnp.concatenate(
        [jnp.dot(p, a[i:i + n, :], preferred_element_type=F32).astype(BF16)
         for i in range(0, a.shape[0], n)], axis=0)


def _previous_token(a, carry_group):
    rows, d = a.shape
    first_row = lax.broadcasted_iota(jnp.int32, (SUBLANES, d), 0) == 0
    pieces = []
    prev_last = carry_group
    for blk in range(rows // QB):
        base = blk * QB
        last = a[base + QB - SUBLANES:base + QB, :]
        pieces.append(jnp.where(first_row, pltpu.roll(prev_last, 1, 0), pltpu.roll(last, 1, 0)))
        pieces.append(a[base:base + QB - SUBLANES, :])
        prev_last = last
    return jnp.concatenate(pieces, axis=0)


def _proj_kernel(x_ref, perm_ref, g_ref, w_ref, b_ref, cw_ref,
                 q_ref, k_ref, v_ref, za_ref, ac_ref, gc_ref, ga_ref, carry):
    rows, d = x_ref.shape

    @pl.when(pl.program_id(1) == 0)
    def _():
        carry[...] = jnp.zeros(carry.shape, F32)

    for r in range(0, rows, PERM_ROWS):
        chunk = slice(r, r + PERM_ROWS)
        u = _permute_rows(perm_ref[...],
                          _rms_norm(x_ref[chunk, :], g_ref[...]).astype(BF16))

        def proj(c, u=u):
            return jnp.dot(u, w_ref[:, c * d:(c + 1) * d], preferred_element_type=F32)

        a = proj(2) * proj(0)
        back1 = _previous_token(a, carry[SUBLANES:2 * SUBLANES, :])
        back2 = _previous_token(back1, carry[0:SUBLANES, :])
        carry[...] = a[PERM_ROWS - 2 * SUBLANES:PERM_ROWS, :]
        c = cw_ref[2:3, :] * a + cw_ref[1:2, :] * back1 + cw_ref[0:1, :] * back2
        ac_ref[chunk, :] = (jax.nn.silu(proj(3)) * proj(1) * c).astype(BF16)

        za_ref[chunk, :] = jax.nn.silu(proj(7)).astype(BF16)
        gc_ref[chunk, :] = jax.nn.sigmoid(proj(8) + b_ref[:, 0:d]).astype(BF16)
        ga_ref[chunk, :] = jax.nn.sigmoid(proj(9) + b_ref[:, d:2 * d]).astype(BF16)
        q_ref[chunk, :] = proj(4) * (HEAD_DIM ** -0.5 * LOG2_E)
        k_ref[chunk, :] = proj(5)
        v_ref[chunk, :] = proj(6)


def _proj_stage(x, norm_g, w_in, b_merge, conv_w):
    bsz, seq, d = x.shape
    rows = PROJ_ROWS
    tile = pl.BlockSpec((None, rows, d), lambda b, i: (b, i, 0))
    const = lambda shape: pl.BlockSpec(shape, lambda b, i: (0,) * len(shape),
                                       pipeline_mode=pl.Buffered(1))
    wide = jax.ShapeDtypeStruct((bsz, seq, d), F32)
    narrow = jax.ShapeDtypeStruct((bsz, seq, d), BF16)
    return pl.pallas_call(
        _proj_kernel,
        grid=(bsz, seq // rows),
        in_specs=[tile, const((PERM_ROWS, PERM_ROWS)), const((1, d)), const(w_in.shape),
                  const((1, 2 * d)), const((CONV_WIDTH, d))],
        out_specs=[tile] * 7,
        out_shape=[wide] * 3 + [narrow] * 4,
        scratch_shapes=[pltpu.VMEM((2 * SUBLANES, d), F32)],
        compiler_params=pltpu.CompilerParams(
            dimension_semantics=("arbitrary", "arbitrary"),
            vmem_limit_bytes=VMEM_LIMIT_BYTES),
        name="proj_stage",
    )(x, _to_storage_order(PERM_ROWS), norm_g, w_in, b_merge, conv_w)


def _zero_first_predecessors(bufs, n_chains, n_blocks):
    for buf in bufs:
        for c in range(n_chains):
            buf[c * n_blocks] = jnp.zeros(buf.shape[1:], buf.dtype)


def _attn_kernel(bias_ref, q_ref, k_ref, v_ref, o_ref, stat_ref, kbuf, vbuf, *,
                 lead, chained):
    n_chains = q_ref.shape[-3] if chained else 1

    @pl.when(pl.program_id(2) == 0)
    def _():
        _zero_first_predecessors((kbuf, vbuf), n_chains, int(np.prod(lead)))

    _attend(bias_ref, q_ref, k_ref, v_ref, o_ref, stat_ref, kbuf, vbuf, lead, chained)


def _attn_pair_kernel(bias_a, bias_b, q_ref, k_ref, v_ref, o_a, stat_a, o_b, stat_b,
                      kbuf_a, vbuf_a, kbuf_b, vbuf_b):
    @pl.when(pl.program_id(2) == 0)
    def _():
        _zero_first_predecessors((kbuf_a, vbuf_a), 1, 4)
        _zero_first_predecessors((kbuf_b, vbuf_b), 4, 1)

    _attend(bias_a, q_ref, k_ref, v_ref, o_a, stat_a, kbuf_a, vbuf_a, (4,), False)
    _attend(bias_b, q_ref, k_ref, v_ref, o_b, stat_b, kbuf_b, vbuf_b, (), True)


def _attend(bias_ref, q_ref, k_ref, v_ref, o_ref, stat_ref, kbuf, vbuf, lead, chained):
    n_blocks, d = int(np.prod(lead)), q_ref.shape[-1]
    n_chains = q_ref.shape[-3] if chained else 1
    first = (pl.program_id(2) == 0).astype(jnp.int32)

    lane = lax.broadcasted_iota(jnp.int32, (QB, LANES), 1)
    head_masks = [(lane // HEAD_DIM) == e for e in range(HEADS_PER_LANE_TILE)]
    rows = (slice(None),) * 2

    for c, g in [(c, g) for c in range(n_chains) for g in range(n_blocks)]:
        chain = (c,) if chained else ()
        where = tuple(int(i) for i in np.unravel_index(g, lead))
        slot = c * n_blocks + g
        keep = c * n_blocks + (g + 1) % n_blocks
        stat_tile = jnp.zeros((QB, LANES), F32)
        for t in range(N_HEADS // HEADS_PER_LANE_TILE):
            cols = slice(t * LANES, (t + 1) * LANES)
            at = where + rows + chain + (slice(None), cols)
            block = lambda ref: ref[at].reshape(QB, LANES).astype(BF16)
            q, k_cur, v_cur = block(q_ref), block(k_ref), block(v_ref)
            k = jnp.concatenate([kbuf[slot, :, cols], k_cur], axis=0)
            v = jnp.concatenate([vbuf[slot, :, cols], v_cur], axis=0)
            kbuf[keep, :, cols] = k_cur
            vbuf[keep, :, cols] = v_cur
            q_heads = jnp.concatenate(
                [jnp.where(m, q, jnp.zeros_like(q)) for m in head_masks], axis=0)
            s_all = lax.dot_general(q_heads, k, (((1,), (1,)), ((), ())),
                                    preferred_element_type=F32)
            ps = []
            for e in range(HEADS_PER_LANE_TILE):
                h = t * HEADS_PER_LANE_TILE + e
                s = s_all[e * QB:(e + 1) * QB, :] + bias_ref[first if g == 0 else 0, h]
                m = jnp.max(s, axis=-1, keepdims=True)
                p = jnp.exp2(s - m)
                stat_tile = jnp.where(lane == h, m, stat_tile)
                ps.append(p.astype(BF16))
            acc_all = jnp.dot(jnp.concatenate(ps, axis=0),
                              jnp.concatenate([v, jnp.ones_like(v)], axis=1),
                              preferred_element_type=F32)
            o_tile = acc_all[0:QB, 0:LANES]
            for e in range(HEADS_PER_LANE_TILE):
                h = t * HEADS_PER_LANE_TILE + e
                mine = acc_all[e * QB:(e + 1) * QB, :]
                stat_tile = jnp.where(lane == N_HEADS + h, mine[:, LANES:], stat_tile)
                if e:
                    o_tile = jnp.where(head_masks[e], mine[:, 0:LANES], o_tile)
            o_ref[at] = o_tile.reshape(4, 4, SUBLANES, LANES)
        stat_at = where + rows + chain
        stat_ref[stat_at] = stat_tile.reshape(4, 4, SUBLANES, LANES)


def _block_positions(dilation):
    row = np.arange(QB)
    if dilation == 16:
        return row
    if dilation == 4:
        return (row // 32) * 32 + (row % 8) * 4 + (row // 8) % 4
    if dilation == 1:
        return (row % 8) * 16 + row // 8
    raise NotImplementedError(dilation)


def _alibi_bias(dilation):
    pos = _block_positions(dilation)
    q_loc = pos[:, None] + QB
    k_loc = np.concatenate([pos, pos + QB])[None, :]
    delta = q_loc - k_loc
    valid = (delta >= 0) & (delta <= QB)
    has_prev = np.stack([np.ones((2 * QB,), bool), np.arange(2 * QB) >= QB])
    ok = valid[None, None] & has_prev[:, None, None, :]
    slopes = np.exp2(-8.0 * np.arange(1, N_HEADS + 1) / N_HEADS)
    bias = -slopes[:, None, None] * (dilation * delta)[None] * LOG2_E
    return jnp.asarray(np.where(ok, bias[None], -np.inf), F32)


def _box_spec(dilation, width, g, chains):
    if dilation == 16 and chains > 1:
        low = 4 // chains
        return pl.BlockSpec((None, g, 4, 4, None, chains, SUBLANES, width),
                            lambda b, r, i: (b, i, 0, 0, r // low, r % low, 0, 0))
    assert chains == 1
    if dilation == 16:
        return pl.BlockSpec((None, g, 4, 4, None, None, SUBLANES, width),
                            lambda b, r, i: (b, i, 0, 0, r // 4, r % 4, 0, 0))
    if g > 4:
        hi = g // 4
        if dilation == 4:
            return pl.BlockSpec((None, hi, 4, 4, 4, None, SUBLANES, width),
                                lambda b, r, i: (b, i, 0, 0, 0, r, 0, 0))
        if dilation == 1:
            per = 4 // hi
            return pl.BlockSpec((None, None, hi, 4, 4, 4, SUBLANES, width),
                                lambda b, r, i: (b, i // per, i % per, 0, 0, 0, 0, 0))
    per = 4 // g
    if dilation == 4:
        return pl.BlockSpec((None, None, g, 4, 4, None, SUBLANES, width),
                            lambda b, r, i: (b, i // per, i % per, 0, 0, r, 0, 0))
    if dilation == 1:
        return pl.BlockSpec((None, None, None, g, 4, 4, SUBLANES, width),
                            lambda b, r, i: (b, i // (4 * per), (i // per) % 4, i % per, 0, 0, 0, 0))
    raise NotImplementedError(dilation)


def _attn_stage(q, k, v, dilation):
    bsz, seq, d = q.shape
    nb = seq // (dilation * QB)
    view = lambda a: a.reshape(bsz, seq // 2048, 4, 4, 4, 4, SUBLANES, a.shape[-1])
    bias = _alibi_bias(dilation)
    bias_spec = pl.BlockSpec(bias.shape, lambda b, r, i: (0, 0, 0, 0),
                             pipeline_mode=pl.Buffered(1))
    g = min(ATTN_BLOCKS, nb)
    chains = min(ATTN_BLOCKS // g, 4) if dilation == 16 else 1
    lead = (g,) if g <= 4 or dilation == 16 else (g // 4, 4)
    box = _box_spec(dilation, d, g, chains)
    stat_box = _box_spec(dilation, LANES, g, chains)
    stat_shape = jax.ShapeDtypeStruct(view(q).shape[:-1] + (LANES,), F32)
    o, stat = pl.pallas_call(
        functools.partial(_attn_kernel, lead=lead, chained=chains > 1),
        grid=(bsz, dilation // chains, nb // g),
        in_specs=[bias_spec, box, box, box],
        out_specs=[box, stat_box],
        out_shape=[jax.ShapeDtypeStruct(view(q).shape, F32), stat_shape],
        scratch_shapes=[pltpu.VMEM((chains * g, QB, d), BF16)] * 2,
        compiler_params=pltpu.CompilerParams(
            dimension_semantics=("arbitrary", "arbitrary", "arbitrary"),
            vmem_limit_bytes=VMEM_LIMIT_BYTES),
        name=f"attn_stage_d{dilation}",
    )(bias, view(q), view(k), view(v))
    return o.reshape(bsz * seq, d), stat.reshape(bsz * seq, LANES)


def _attn_pair_stage(q, k, v):
    bsz, seq, d = q.shape
    view = lambda a: a.reshape(bsz, seq // 2048, 4, 4, 4, 4, SUBLANES, a.shape[-1])
    box = lambda w: pl.BlockSpec((None, None, 4, 4, 4, None, SUBLANES, w),
                                 lambda b, r, a: (b, a, 0, 0, 0, r, 0, 0))
    biases = [_alibi_bias(4), _alibi_bias(16)]
    bias_spec = pl.BlockSpec(biases[0].shape, lambda b, r, a: (0, 0, 0, 0),
                             pipeline_mode=pl.Buffered(1))
    o_shape = jax.ShapeDtypeStruct(view(q).shape, F32)
    stat_shape = jax.ShapeDtypeStruct(view(q).shape[:-1] + (LANES,), F32)
    outs = pl.pallas_call(
        _attn_pair_kernel,
        grid=(bsz, 4, seq // 2048),
        in_specs=[bias_spec, bias_spec, box(d), box(d), box(d)],
        out_specs=[box(d), box(LANES)] * 2,
        out_shape=[o_shape, stat_shape] * 2,
        scratch_shapes=[pltpu.VMEM((4, QB, d), BF16)] * 4,
        compiler_params=pltpu.CompilerParams(
            dimension_semantics=("arbitrary", "arbitrary", "arbitrary"),
            vmem_limit_bytes=VMEM_LIMIT_BYTES),
        name="attn_stage_d4_d16",
    )(*biases, view(q), view(k), view(v))
    flat = [a.reshape(bsz * seq, a.shape[-1]) for a in outs]
    return (flat[0], flat[1]), (flat[2], flat[3])


def _out_kernel(*refs, n_patterns):
    x_ref, ac_ref, za_ref, gc_ref, ga_ref = refs[:5]
    o_refs = refs[5:5 + n_patterns]
    stat_refs = refs[5 + n_patterns:5 + 2 * n_patterns]
    spread_ref, perm_ref, woc_ref, woa_ref, wo_ref, fg_ref, y_ref = refs[5 + 2 * n_patterns:]

    ms = [r[...] for r in stat_refs]
    lane = lax.broadcasted_iota(jnp.int32, ms[0].shape, 1)
    ls = [jnp.where(lane < N_HEADS, pltpu.roll(m, LANES - N_HEADS, 1), 1.0) for m in ms]
    top = functools.reduce(jnp.maximum, ms)
    ws = [jnp.exp2(m - top) for m in ms]
    total = functools.reduce(jnp.add, [w * l for w, l in zip(ws, ls)])
    mixed_o = None
    for w, o_ref in zip(ws, o_refs):
        alpha = w / total
        hi = alpha.astype(BF16)
        lo = (alpha - hi.astype(F32)).astype(BF16)
        wide = jnp.dot(jnp.concatenate([hi, lo], axis=1), spread_ref[...],
                       preferred_element_type=F32)
        term = wide * o_ref[...]
        mixed_o = term if mixed_o is None else mixed_o + term
    gated = (za_ref[...].astype(F32) * mixed_o).astype(BF16)
    y_attn = jnp.dot(gated, woa_ref[...], preferred_element_type=F32)
    y_conv = jnp.dot(ac_ref[...], woc_ref[...], preferred_element_type=F32)
    merged = gc_ref[...].astype(F32) * y_conv + ga_ref[...].astype(F32) * y_attn
    merged = _permute_rows(perm_ref[...], merged.astype(BF16))
    for r in range(0, merged.shape[0], PERM_ROWS):
        chunk = slice(r, r + PERM_ROWS)
        mixed = jnp.dot(merged[chunk, :], wo_ref[...], preferred_element_type=F32)
        y_ref[chunk, :] = _rms_norm(x_ref[chunk, :] + mixed, fg_ref[...])


def _head_spread(d):
    s = np.zeros((2 * LANES, d), np.float32)
    for h in range(N_HEADS):
        s[h, h * HEAD_DIM:(h + 1) * HEAD_DIM] = 1.0
        s[LANES + h, h * HEAD_DIM:(h + 1) * HEAD_DIM] = 1.0
    return jnp.asarray(s, BF16)


def _out_stage(x, ac, za, gc, ga, os_, stats, w_out_conv, w_out_attn, w_o, final_g):
    n, d = x.shape
    rows = OUT_ROWS
    n_patterns = len(os_)
    tile = pl.BlockSpec((rows, d), lambda i: (i, 0))
    stat = pl.BlockSpec((rows, LANES), lambda i: (i, 0))
    const = lambda shape: pl.BlockSpec(shape, lambda i: (0,) * len(shape),
                                       pipeline_mode=pl.Buffered(1))
    return pl.pallas_call(
        functools.partial(_out_kernel, n_patterns=n_patterns),
        grid=(n // rows,),
        in_specs=([tile] * (5 + n_patterns) + [stat] * n_patterns
                  + [const((2 * LANES, d)), const((PERM_ROWS, PERM_ROWS))]
                  + [const((d, d))] * 3 + [const((1, d))]),
        out_specs=tile,
        out_shape=jax.ShapeDtypeStruct((n, d), F32),
        compiler_params=pltpu.CompilerParams(
            dimension_semantics=("arbitrary",),
            vmem_limit_bytes=VMEM_LIMIT_BYTES),
        name="out_stage",
    )(x, ac, za, gc, ga, *os_, *stats, _head_spread(d), _to_storage_order(PERM_ROWS).T,
      w_out_conv, w_out_attn, w_o, final_g)


def _layer(h, norm_g, w_in, b_merge, conv_w, w_out_conv, w_out_attn, w_o, out_g):
    bsz, seq, d = h.shape
    q, k, v, za, ac, gc, ga = _proj_stage(
        h, norm_g[None, :], w_in.astype(BF16), b_merge[None, :], conv_w)
    os_, stats = zip(_attn_stage(q, k, v, 1), *_attn_pair_stage(q, k, v))
    flat = lambda a: a.reshape(bsz * seq, d)
    y = _out_stage(flat(h), flat(ac), flat(za), flat(gc), flat(ga), os_, stats,
                   w_out_conv.astype(BF16), w_out_attn.astype(BF16), w_o.astype(BF16),
                   out_g[None, :])
    return y.reshape(bsz, seq, d)


def kernel(x, norm_g, w_in, b_merge, conv_w, w_out_conv, w_out_attn, w_o, final_g):
    depth = norm_g.shape[0]
    assert depth == 1, "the fused output stage applies the final norm after a single layer"
    assert x.shape[-1] == N_HEADS * HEAD_DIM and x.shape[1] % 4096 == 0
    assert ATTN_PATTERNS == ((128, 1), (512, 4), (2048, 16)), "box specs are per pattern"
    return _layer(x, norm_g[0], w_in[0], b_merge[0], conv_w[0],
                  w_out_conv[0], w_out_attn[0], w_o[0], final_g)
```

```python
import functools

import jax
import jax.numpy as jnp
import numpy as np
from jax import lax
from jax.experimental import pallas as pl
from jax.experimental.pallas import tpu as pltpu

N_HEADS = 16
HEAD_DIM = 64
CONV_WIDTH = 3
ATTN_PATTERNS = ((128, 1), (512, 4), (2048, 16))
QB = 128
EPS = 1e-6
LOG2_E = 1.4426950408889634

LANES = 128
SUBLANES = 8
HEADS_PER_LANE_TILE = LANES // HEAD_DIM
GROUPS = QB // SUBLANES
ATTN_BLOCKS = 8
PERM_ROWS = 256
PROJ_ROWS = 256
OUT_ROWS = 512
VMEM_LIMIT_BYTES = 56 * 1024 * 1024

BF16 = jnp.bfloat16
F32 = jnp.float32


def _rms_norm(x, g):
    return x * lax.rsqrt(jnp.mean(x * x, axis=-1, keepdims=True) + EPS) * g


def _to_storage_order(rows):
    s = np.arange(rows)
    natural = (s // QB) * QB + (s % SUBLANES) * GROUPS + (s % QB) // SUBLANES
    p = np.zeros((rows, rows), np.float32)
    p[s, natural] = 1.0
    return jnp.asarray(p, BF16)


def _permute_rows(p, a):
    n = p.shape[0]
    return jnp.concatenate(
        [jnp.dot(p, a[i:i + n, :], preferred_element_type=F32).astype(BF16)
         for i in range(0, a.shape[0], n)], axis=0)


def _previous_token(a, carry_group):
    rows, d = a.shape
    first_row = lax.broadcasted_iota(jnp.int32, (SUBLANES, d), 0) == 0
    pieces = []
    prev_last = carry_group
    for blk in range(rows // QB):
        base = blk * QB
        last = a[base + QB - SUBLANES:base + QB, :]
        pieces.append(jnp.where(first_row, pltpu.roll(prev_last, 1, 0), pltpu.roll(last, 1, 0)))
        pieces.append(a[base:base + QB - SUBLANES, :])
        prev_last = last
    return jnp.concatenate(pieces, axis=0)


def _proj_kernel(x_ref, perm_ref, g_ref, w_ref, b_ref, cw_ref,
                 q_ref, k_ref, v_ref, za_ref, ac_ref, gc_ref, ga_ref,
                 qn_ref, kn_ref, vn_ref, carry):
    rows, d = x_ref.shape

    @pl.when(pl.program_id(1) == 0)
    def _():
        carry[...] = jnp.zeros(carry.shape, F32)

    for r in range(0, rows, PERM_ROWS):
        chunk = slice(r, r + PERM_ROWS)
        u = _permute_rows(perm_ref[...],
                          _rms_norm(x_ref[chunk, :], g_ref[...]).astype(BF16))

        def proj(c, u=u):
            return jnp.dot(u, w_ref[:, c * d:(c + 1) * d], preferred_element_type=F32)

        a = proj(2) * proj(0)
        back1 = _previous_token(a, carry[SUBLANES:2 * SUBLANES, :])
        back2 = _previous_token(back1, carry[0:SUBLANES, :])
        carry[...] = a[PERM_ROWS - 2 * SUBLANES:PERM_ROWS, :]
        c = cw_ref[2:3, :] * a + cw_ref[1:2, :] * back1 + cw_ref[0:1, :] * back2
        ac_ref[chunk, :] = (jax.nn.silu(proj(3)) * proj(1) * c).astype(BF16)

        za_ref[chunk, :] = jax.nn.silu(proj(7)).astype(BF16)
        gc_ref[chunk, :] = jax.nn.sigmoid(proj(8) + b_ref[:, 0:d]).astype(BF16)
        ga_ref[chunk, :] = jax.nn.sigmoid(proj(9) + b_ref[:, d:2 * d]).astype(BF16)
        for col, wide_ref, narrow_ref, scale in ((4, q_ref, qn_ref, HEAD_DIM ** -0.5 * LOG2_E),
                                                 (5, k_ref, kn_ref, None),
                                                 (6, v_ref, vn_ref, None)):
            val = proj(col) if scale is None else proj(col) * scale
            wide_ref[chunk, :] = val
            narrow_ref[chunk, :] = val.astype(BF16)


def _proj_stage(x, norm_g, w_in, b_merge, conv_w):
    bsz, seq, d = x.shape
    rows = PROJ_ROWS
    tile = pl.BlockSpec((None, rows, d), lambda b, i: (b, i, 0))
    const = lambda shape: pl.BlockSpec(shape, lambda b, i: (0,) * len(shape),
                                       pipeline_mode=pl.Buffered(1))
    wide = jax.ShapeDtypeStruct((bsz, seq, d), F32)
    narrow = jax.ShapeDtypeStruct((bsz, seq, d), BF16)
    return pl.pallas_call(
        _proj_kernel,
        grid=(bsz, seq // rows),
        in_specs=[tile, const((PERM_ROWS, PERM_ROWS)), const((1, d)), const(w_in.shape),
                  const((1, 2 * d)), const((CONV_WIDTH, d))],
        out_specs=[tile] * 10,
        out_shape=[wide] * 3 + [narrow] * 7,
        scratch_shapes=[pltpu.VMEM((2 * SUBLANES, d), F32)],
        compiler_params=pltpu.CompilerParams(
            dimension_semantics=("arbitrary", "arbitrary"),
            vmem_limit_bytes=VMEM_LIMIT_BYTES),
        name="proj_stage",
    )(x, _to_storage_order(PERM_ROWS), norm_g, w_in, b_merge, conv_w)


def _zero_first_predecessors(bufs, n_chains, n_blocks):
    for buf in bufs:
        for c in range(n_chains):
            buf[c * n_blocks] = jnp.zeros(buf.shape[1:], buf.dtype)


def _attn_kernel(bias_ref, q_ref, k_ref, v_ref, o_ref, stat_ref, kbuf, vbuf, *,
                 lead, chained):
    n_chains = o_ref.shape[-3] if chained else 1

    @pl.when(pl.program_id(2) == 0)
    def _():
        _zero_first_predecessors((kbuf, vbuf), n_chains, int(np.prod(lead)))

    _attend(bias_ref, q_ref, k_ref, v_ref, o_ref, stat_ref, kbuf, vbuf, lead, chained)


def _attn_pair_kernel(bias_a, bias_b, q_ref, k_ref, v_ref, o_a, stat_a, o_b, stat_b,
                      kbuf_a, vbuf_a, kbuf_b, vbuf_b):
    @pl.when(pl.program_id(2) == 0)
    def _():
        _zero_first_predecessors((kbuf_a, vbuf_a), 1, 4)
        _zero_first_predecessors((kbuf_b, vbuf_b), 4, 1)

    _attend(bias_a, q_ref, k_ref, v_ref, o_a, stat_a, kbuf_a, vbuf_a, (4,), False)
    _attend(bias_b, q_ref, k_ref, v_ref, o_b, stat_b, kbuf_b, vbuf_b, (), True)


def _attend(bias_ref, q_ref, k_ref, v_ref, o_ref, stat_ref, kbuf, vbuf, lead, chained):
    n_blocks, d = int(np.prod(lead)), q_ref.shape[-1]
    n_chains = o_ref.shape[-3] if chained else 1
    flat_inputs = len(q_ref.shape) == 3
    first = (pl.program_id(2) == 0).astype(jnp.int32)

    lane = lax.broadcasted_iota(jnp.int32, (QB, LANES), 1)
    head_masks = [(lane // HEAD_DIM) == e for e in range(HEADS_PER_LANE_TILE)]
    rows = (slice(None),) * 2

    for c, g in [(c, g) for c in range(n_chains) for g in range(n_blocks)]:
        chain = (c,) if chained else ()
        where = tuple(int(i) for i in np.unravel_index(g, lead))
        slot = c * n_blocks + g
        keep = c * n_blocks + (g + 1) % n_blocks
        stat_tile = jnp.zeros((QB, LANES), F32)
        for t in range(N_HEADS // HEADS_PER_LANE_TILE):
            cols = slice(t * LANES, (t + 1) * LANES)
            at = where + rows + chain + (slice(None), cols)
            if flat_inputs:
                block = lambda ref: ref[g, :, cols]
            else:
                block = lambda ref: ref[at].reshape(QB, LANES).astype(BF16)
            q, k_cur, v_cur = block(q_ref), block(k_ref), block(v_ref)
            k = jnp.concatenate([kbuf[slot, :, cols], k_cur], axis=0)
            v = jnp.concatenate([vbuf[slot, :, cols], v_cur], axis=0)
            kbuf[keep, :, cols] = k_cur
            vbuf[keep, :, cols] = v_cur
            q_heads = jnp.concatenate(
                [jnp.where(m, q, jnp.zeros_like(q)) for m in head_masks], axis=0)
            s_all = lax.dot_general(q_heads, k, (((1,), (1,)), ((), ())),
                                    preferred_element_type=F32)
            ps = []
            for e in range(HEADS_PER_LANE_TILE):
                h = t * HEADS_PER_LANE_TILE + e
                s = s_all[e * QB:(e + 1) * QB, :] + bias_ref[first if g == 0 else 0, h]
                m = jnp.max(s, axis=-1, keepdims=True)
                p = jnp.exp2(s - m)
                stat_tile = jnp.where(lane == h, m, stat_tile)
                ps.append(p.astype(BF16))
            acc_all = jnp.dot(jnp.concatenate(ps, axis=0),
                              jnp.concatenate([v, jnp.ones_like(v)], axis=1),
                              preferred_element_type=F32)
            o_tile = acc_all[0:QB, 0:LANES]
            for e in range(HEADS_PER_LANE_TILE):
                h = t * HEADS_PER_LANE_TILE + e
                mine = acc_all[e * QB:(e + 1) * QB, :]
                stat_tile = jnp.where(lane == N_HEADS + h, mine[:, LANES:], stat_tile)
                if e:
                    o_tile = jnp.where(head_masks[e], mine[:, 0:LANES], o_tile)
            o_ref[at] = o_tile.reshape(4, 4, SUBLANES, LANES)
        stat_at = where + rows + chain
        stat_ref[stat_at] = stat_tile.reshape(4, 4, SUBLANES, LANES)


def _block_positions(dilation):
    row = np.arange(QB)
    if dilation == 16:
        return row
    if dilation == 4:
        return (row // 32) * 32 + (row % 8) * 4 + (row // 8) % 4
    if dilation == 1:
        return (row % 8) * 16 + row // 8
    raise NotImplementedError(dilation)


def _alibi_bias(dilation):
    pos = _block_positions(dilation)
    q_loc = pos[:, None] + QB
    k_loc = np.concatenate([pos, pos + QB])[None, :]
    delta = q_loc - k_loc
    valid = (delta >= 0) & (delta <= QB)
    has_prev = np.stack([np.ones((2 * QB,), bool), np.arange(2 * QB) >= QB])
    ok = valid[None, None] & has_prev[:, None, None, :]
    slopes = np.exp2(-8.0 * np.arange(1, N_HEADS + 1) / N_HEADS)
    bias = -slopes[:, None, None] * (dilation * delta)[None] * LOG2_E
    return jnp.asarray(np.where(ok, bias[None], -np.inf), F32)


def _box_spec(dilation, width, g, chains):
    if dilation == 16 and chains > 1:
        low = 4 // chains
        return pl.BlockSpec((None, g, 4, 4, None, chains, SUBLANES, width),
                            lambda b, r, i: (b, i, 0, 0, r // low, r % low, 0, 0))
    assert chains == 1
    if dilation == 16:
        return pl.BlockSpec((None, g, 4, 4, None, None, SUBLANES, width),
                            lambda b, r, i: (b, i, 0, 0, r // 4, r % 4, 0, 0))
    if g > 4:
        hi = g // 4
        if dilation == 4:
            return pl.BlockSpec((None, hi, 4, 4, 4, None, SUBLANES, width),
                                lambda b, r, i: (b, i, 0, 0, 0, r, 0, 0))
        if dilation == 1:
            per = 4 // hi
            return pl.BlockSpec((None, None, hi, 4, 4, 4, SUBLANES, width),
                                lambda b, r, i: (b, i // per, i % per, 0, 0, 0, 0, 0))
    per = 4 // g
    if dilation == 4:
        return pl.BlockSpec((None, None, g, 4, 4, None, SUBLANES, width),
                            lambda b, r, i: (b, i // per, i % per, 0, 0, r, 0, 0))
    if dilation == 1:
        return pl.BlockSpec((None, None, None, g, 4, 4, SUBLANES, width),
                            lambda b, r, i: (b, i // (4 * per), (i // per) % 4, i % per, 0, 0, 0, 0))
    raise NotImplementedError(dilation)


def _attn_stage(q, k, v, dilation, narrow=None):
    bsz, seq, d = q.shape
    nb = seq // (dilation * QB)
    view = lambda a: a.reshape(bsz, seq // 2048, 4, 4, 4, 4, SUBLANES, a.shape[-1])
    bias = _alibi_bias(dilation)
    bias_spec = pl.BlockSpec(bias.shape, lambda b, r, i: (0, 0, 0, 0),
                             pipeline_mode=pl.Buffered(1))
    g = min(ATTN_BLOCKS, nb)
    chains = min(ATTN_BLOCKS // g, 4) if dilation == 16 else 1
    lead = (g,) if g <= 4 or dilation == 16 else (g // 4, 4)
    box = _box_spec(dilation, d, g, chains)
    stat_box = _box_spec(dilation, LANES, g, chains)
    stat_shape = jax.ShapeDtypeStruct(view(q).shape[:-1] + (LANES,), F32)
    if narrow is None:
        in_box, operands = box, [view(q), view(k), view(v)]
    else:
        assert dilation == 1
        in_box = pl.BlockSpec((None, g, QB, d), lambda b, r, i: (b, i, 0, 0))
        operands = [a.reshape(bsz, seq // QB, QB, d) for a in narrow]
    o, stat = pl.pallas_call(
        functools.partial(_attn_kernel, lead=lead, chained=chains > 1),
        grid=(bsz, dilation // chains, nb // g),
        in_specs=[bias_spec, in_box, in_box, in_box],
        out_specs=[box, stat_box],
        out_shape=[jax.ShapeDtypeStruct(view(q).shape, F32), stat_shape],
        scratch_shapes=[pltpu.VMEM((chains * g, QB, d), BF16)] * 2,
        compiler_params=pltpu.CompilerParams(
            dimension_semantics=("arbitrary", "arbitrary", "arbitrary"),
            vmem_limit_bytes=VMEM_LIMIT_BYTES),
        name=f"attn_stage_d{dilation}",
    )(bias, *operands)
    return o.reshape(bsz * seq, d), stat.reshape(bsz * seq, LANES)


def _attn_pair_stage(q, k, v):
    bsz, seq, d = q.shape
    view = lambda a: a.reshape(bsz, seq // 2048, 4, 4, 4, 4, SUBLANES, a.shape[-1])
    box = lambda w: pl.BlockSpec((None, None, 4, 4, 4, None, SUBLANES, w),
                                 lambda b, r, a: (b, a, 0, 0, 0, r, 0, 0))
    biases = [_alibi_bias(4), _alibi_bias(16)]
    bias_spec = pl.BlockSpec(biases[0].shape, lambda b, r, a: (0, 0, 0, 0),
                             pipeline_mode=pl.Buffered(1))
    o_shape = jax.ShapeDtypeStruct(view(q).shape, F32)
    stat_shape = jax.ShapeDtypeStruct(view(q).shape[:-1] + (LANES,), F32)
    outs = pl.pallas_call(
        _attn_pair_kernel,
        grid=(bsz, 4, seq // 2048),
        in_specs=[bias_spec, bias_spec, box(d), box(d), box(d)],
        out_specs=[box(d), box(LANES)] * 2,
        out_shape=[o_shape, stat_shape] * 2,
        scratch_shapes=[pltpu.VMEM((4, QB, d), BF16)] * 4,
        compiler_params=pltpu.CompilerParams(
            dimension_semantics=("arbitrary", "arbitrary", "arbitrary"),
            vmem_limit_bytes=VMEM_LIMIT_BYTES),
        name="attn_stage_d4_d16",
    )(*biases, view(q), view(k), view(v))
    flat = [a.reshape(bsz * seq, a.shape[-1]) for a in outs]
    return (flat[0], flat[1]), (flat[2], flat[3])


def _out_kernel(*refs, n_patterns):
    x_ref, ac_ref, za_ref, gc_ref, ga_ref = refs[:5]
    o_refs = refs[5:5 + n_patterns]
    stat_refs = refs[5 + n_patterns:5 + 2 * n_patterns]
    spread_ref, perm_ref, woc_ref, woa_ref, wo_ref, fg_ref, y_ref = refs[5 + 2 * n_patterns:]

    ms = [r[...] for r in stat_refs]
    lane = lax.broadcasted_iota(jnp.int32, ms[0].shape, 1)
    ls = [jnp.where(lane < N_HEADS, pltpu.roll(m, LANES - N_HEADS, 1), 1.0) for m in ms]
    top = functools.reduce(jnp.maximum, ms)
    ws = [jnp.exp2(m - top) for m in ms]
    total = functools.reduce(jnp.add, [w * l for w, l in zip(ws, ls)])
    mixed_o = None
    for w, o_ref in zip(ws, o_refs):
        alpha = w / total
        hi = alpha.astype(BF16)
        lo = (alpha - hi.astype(F32)).astype(BF16)
        wide = jnp.dot(jnp.concatenate([hi, lo], axis=1), spread_ref[...],
                       preferred_element_type=F32)
        term = wide * o_ref[...]
        mixed_o = term if mixed_o is None else mixed_o + term
    gated = (za_ref[...].astype(F32) * mixed_o).astype(BF16)
    y_attn = jnp.dot(gated, woa_ref[...], preferred_element_type=F32)
    y_conv = jnp.dot(ac_ref[...], woc_ref[...], preferred_element_type=F32)
    merged = gc_ref[...].astype(F32) * y_conv + ga_ref[...].astype(F32) * y_attn
    merged = _permute_rows(perm_ref[...], merged.astype(BF16))
    for r in range(0, merged.shape[0], PERM_ROWS):
        chunk = slice(r, r + PERM_ROWS)
        mixed = jnp.dot(merged[chunk, :], wo_ref[...], preferred_element_type=F32)
        y_ref[chunk, :] = _rms_norm(x_ref[chunk, :] + mixed, fg_ref[...])


def _head_spread(d):
    s = np.zeros((2 * LANES, d), np.float32)
    for h in range(N_HEADS):
        s[h, h * HEAD_DIM:(h + 1) * HEAD_DIM] = 1.0
        s[LANES + h, h * HEAD_DIM:(h + 1) * HEAD_DIM] = 1.0
    return jnp.asarray(s, BF16)


def _out_stage(x, ac, za, gc, ga, os_, stats, w_out_conv, w_out_attn, w_o, final_g):
    n, d = x.shape
    rows = OUT_ROWS
    n_patterns = len(os_)
    tile = pl.BlockSpec((rows, d), lambda i: (i, 0))
    stat = pl.BlockSpec((rows, LANES), lambda i: (i, 0))
    const = lambda shape: pl.BlockSpec(shape, lambda i: (0,) * len(shape),
                                       pipeline_mode=pl.Buffered(1))
    return pl.pallas_call(
        functools.partial(_out_kernel, n_patterns=n_patterns),
        grid=(n // rows,),
        in_specs=([tile] * (5 + n_patterns) + [stat] * n_patterns
                  + [const((2 * LANES, d)), const((PERM_ROWS, PERM_ROWS))]
                  + [const((d, d))] * 3 + [const((1, d))]),
        out_specs=tile,
        out_shape=jax.ShapeDtypeStruct((n, d), F32),
        compiler_params=pltpu.CompilerParams(
            dimension_semantics=("arbitrary",),
            vmem_limit_bytes=VMEM_LIMIT_BYTES),
        name="out_stage",
    )(x, ac, za, gc, ga, *os_, *stats, _head_spread(d), _to_storage_order(PERM_ROWS).T,
      w_out_conv, w_out_attn, w_o, final_g)


def _layer(h, norm_g, w_in, b_merge, conv_w, w_out_conv, w_out_attn, w_o, out_g):
    bsz, seq, d = h.shape
    q, k, v, za, ac, gc, ga, qn, kn, vn = _proj_stage(
        h, norm_g[None, :], w_in.astype(BF16), b_merge[None, :], conv_w)
    os_, stats = zip(_attn_stage(q, k, v, 1, narrow=(qn, kn, vn)),
                     *_attn_pair_stage(q, k, v))
    flat = lambda a: a.reshape(bsz * seq, d)
    y = _out_stage(flat(h), flat(ac), flat(za), flat(gc), flat(ga), os_, stats,
                   w_out_conv.astype(BF16), w_out_attn.astype(BF16), w_o.astype(BF16),
                   out_g[None, :])
    return y.reshape(bsz, seq, d)


def kernel(x, norm_g, w_in, b_merge, conv_w, w_out_conv, w_out_attn, w_o, final_g):
    depth = norm_g.shape[0]
    assert depth == 1, "the fused output stage applies the final norm after a single layer"
    assert x.shape[-1] == N_HEADS * HEAD_DIM and x.shape[1] % 4096 == 0
    assert ATTN_PATTERNS == ((128, 1), (512, 4), (2048, 16)), "box specs are per pattern"
    return _layer(x, norm_g[0], w_in[0], b_merge[0], conv_w[0],
                  w_out_conv[0], w_out_attn[0], w_o[0], final_g)
```

```python
import functools

import jax
import jax.numpy as jnp
import numpy as np
from jax import lax
from jax.experimental import pallas as pl
from jax.experimental.pallas import tpu as pltpu

N_HEADS = 16
HEAD_DIM = 64
CONV_WIDTH = 3
ATTN_PATTERNS = ((128, 1), (512, 4), (2048, 16))
QB = 128
EPS = 1e-6
LOG2_E = 1.4426950408889634

LANES = 128
SUBLANES = 8
HEADS_PER_LANE_TILE = LANES // HEAD_DIM
GROUPS = QB // SUBLANES
ATTN_BLOCKS = 8
PERM_ROWS = 256
PROJ_ROWS = 256
OUT_ROWS = 512
VMEM_LIMIT_BYTES = 56 * 1024 * 1024

BF16 = jnp.bfloat16
F32 = jnp.float32


def _rms_norm(x, g):
    return x * lax.rsqrt(jnp.mean(x * x, axis=-1, keepdims=True) + EPS) * g


def _to_storage_order(rows):
    s = np.arange(rows)
    natural = (s // QB) * QB + (s % SUBLANES) * GROUPS + (s % QB) // SUBLANES
    p = np.zeros((rows, rows), np.float32)
    p[s, natural] = 1.0
    return jnp.asarray(p, BF16)


def _permute_rows(p, a):
    n = p.shape[0]
    return jnp.concatenate(
        [jnp.dot(p, a[i:i + n, :], preferred_element_type=F32).astype(BF16)
         for i in range(0, a.shape[0], n)], axis=0)


def _previous_token(a, carry_group):
    rows, d = a.shape
    first_row = lax.broadcasted_iota(jnp.int32, (SUBLANES, d), 0) == 0
    pieces = []
    prev_last = carry_group
    for blk in range(rows // QB):
        base = blk * QB
        last = a[base + QB - SUBLANES:base + QB, :]
        pieces.append(jnp.where(first_row, pltpu.roll(prev_last, 1, 0), pltpu.roll(last, 1, 0)))
        pieces.append(a[base:base + QB - SUBLANES, :])
        prev_last = last
    return jnp.concatenate(pieces, axis=0)


def _proj_kernel(x_ref, perm_ref, g_ref, w_ref, b_ref, cw_ref,
                 q_ref, k_ref, v_ref, za_ref, ac_ref, gc_ref, ga_ref,
                 qn_ref, kn_ref, vn_ref, carry):
    rows, d = x_ref.shape

    @pl.when(pl.program_id(1) == 0)
    def _():
        carry[...] = jnp.zeros(carry.shape, F32)

    for r in range(0, rows, PERM_ROWS):
        chunk = slice(r, r + PERM_ROWS)
        u = _permute_rows(perm_ref[...],
                          _rms_norm(x_ref[chunk, :], g_ref[...]).astype(BF16))

        def proj(c, u=u):
            return jnp.dot(u, w_ref[:, c * d:(c + 1) * d], preferred_element_type=F32)

        a = proj(2) * proj(0)
        back1 = _previous_token(a, carry[SUBLANES:2 * SUBLANES, :])
        back2 = _previous_token(back1, carry[0:SUBLANES, :])
        carry[...] = a[PERM_ROWS - 2 * SUBLANES:PERM_ROWS, :]
        c = cw_ref[2:3, :] * a + cw_ref[1:2, :] * back1 + cw_ref[0:1, :] * back2
        ac_ref[chunk, :] = (jax.nn.silu(proj(3)) * proj(1) * c).astype(BF16)

        za_ref[chunk, :] = jax.nn.silu(proj(7)).astype(BF16)
        gc_ref[chunk, :] = jax.nn.sigmoid(proj(8) + b_ref[:, 0:d]).astype(BF16)
        ga_ref[chunk, :] = jax.nn.sigmoid(proj(9) + b_ref[:, d:2 * d]).astype(BF16)
        for col, wide_ref, narrow_ref, scale in ((4, q_ref, qn_ref, HEAD_DIM ** -0.5 * LOG2_E),
                                                 (5, k_ref, kn_ref, None),
                                                 (6, v_ref, vn_ref, None)):
            val = proj(col) if scale is None else proj(col) * scale
            wide_ref[chunk, :] = val
            narrow_ref[chunk, :] = val.astype(BF16)


def _proj_stage(x, norm_g, w_in, b_merge, conv_w):
    bsz, seq, d = x.shape
    rows = PROJ_ROWS
    tile = pl.BlockSpec((None, rows, d), lambda b, i: (b, i, 0))
    const = lambda shape: pl.BlockSpec(shape, lambda b, i: (0,) * len(shape),
                                       pipeline_mode=pl.Buffered(1))
    wide = jax.ShapeDtypeStruct((bsz, seq, d), F32)
    narrow = jax.ShapeDtypeStruct((bsz, seq, d), BF16)
    return pl.pallas_call(
        _proj_kernel,
        grid=(bsz, seq // rows),
        in_specs=[tile, const((PERM_ROWS, PERM_ROWS)), const((1, d)), const(w_in.shape),
                  const((1, 2 * d)), const((CONV_WIDTH, d))],
        out_specs=[tile] * 10,
        out_shape=[wide] * 3 + [narrow] * 7,
        scratch_shapes=[pltpu.VMEM((2 * SUBLANES, d), F32)],
        compiler_params=pltpu.CompilerParams(
            dimension_semantics=("arbitrary", "arbitrary"),
            vmem_limit_bytes=VMEM_LIMIT_BYTES),
        name="proj_stage",
    )(x, _to_storage_order(PERM_ROWS), norm_g, w_in, b_merge, conv_w)


def _zero_first_predecessors(bufs, n_chains, n_blocks):
    for buf in bufs:
        for c in range(n_chains):
            buf[c * n_blocks] = jnp.zeros(buf.shape[1:], buf.dtype)


def _attn_kernel(bias_ref, q_ref, k_ref, v_ref, o_ref, stat_ref, kbuf, vbuf, *,
                 lead, chained):
    n_chains = o_ref.shape[-3] if chained else 1

    @pl.when(pl.program_id(2) == 0)
    def _():
        _zero_first_predecessors((kbuf, vbuf), n_chains, int(np.prod(lead)))

    _attend(bias_ref, q_ref, k_ref, v_ref, o_ref, stat_ref, kbuf, vbuf, lead, chained)


def _attn_pair_kernel(bias_a, bias_b, q_ref, k_ref, v_ref, o_a, stat_a, o_b, stat_b,
                      kbuf_a, vbuf_a, kbuf_b, vbuf_b):
    @pl.when(pl.program_id(2) == 0)
    def _():
        _zero_first_predecessors((kbuf_a, vbuf_a), 1, 4)
        _zero_first_predecessors((kbuf_b, vbuf_b), 4, 1)

    _attend(bias_a, q_ref, k_ref, v_ref, o_a, stat_a, kbuf_a, vbuf_a, (4,), False)
    _attend(bias_b, q_ref, k_ref, v_ref, o_b, stat_b, kbuf_b, vbuf_b, (), True)


def _attend(bias_ref, q_ref, k_ref, v_ref, o_ref, stat_ref, kbuf, vbuf, lead, chained):
    n_blocks, d = int(np.prod(lead)), q_ref.shape[-1]
    n_chains = o_ref.shape[-3] if chained else 1
    flat_inputs = len(q_ref.shape) == 3
    first = (pl.program_id(2) == 0).astype(jnp.int32)

    lane = lax.broadcasted_iota(jnp.int32, (QB, LANES), 1)
    head_masks = [(lane // HEAD_DIM) == e for e in range(HEADS_PER_LANE_TILE)]
    rows = (slice(None),) * 2

    for c, g in [(c, g) for c in range(n_chains) for g in range(n_blocks)]:
        chain = (c,) if chained else ()
        where = tuple(int(i) for i in np.unravel_index(g, lead))
        slot = c * n_blocks + g
        keep = c * n_blocks + (g + 1) % n_blocks
        stat_tile = jnp.zeros((QB, LANES), F32)
        for t in range(N_HEADS // HEADS_PER_LANE_TILE):
            cols = slice(t * LANES, (t + 1) * LANES)
            at = where + rows + chain + (slice(None), cols)
            if flat_inputs:
                block = lambda ref: ref[g, :, cols]
            else:
                block = lambda ref: ref[at].reshape(QB, LANES).astype(BF16)
            q, k_cur, v_cur = block(q_ref), block(k_ref), block(v_ref)
            k = jnp.concatenate([kbuf[slot, :, cols], k_cur], axis=0)
            v = jnp.concatenate([vbuf[slot, :, cols], v_cur], axis=0)
            kbuf[keep, :, cols] = k_cur
            vbuf[keep, :, cols] = v_cur
            q_heads = jnp.concatenate(
                [jnp.where(m, q, jnp.zeros_like(q)) for m in head_masks], axis=0)
            s_all = lax.dot_general(q_heads, k, (((1,), (1,)), ((), ())),
                                    preferred_element_type=F32)
            ps = []
            for e in range(HEADS_PER_LANE_TILE):
                h = t * HEADS_PER_LANE_TILE + e
                s = s_all[e * QB:(e + 1) * QB, :] + bias_ref[first if g == 0 else 0, h]
                m = jnp.max(s, axis=-1, keepdims=True)
                p = jnp.exp2(s - m)
                stat_tile = jnp.where(lane == h, m, stat_tile)
                ps.append(p.astype(BF16))
            acc_all = jnp.dot(jnp.concatenate(ps, axis=0),
                              jnp.concatenate([v, jnp.ones_like(v)], axis=1),
                              preferred_element_type=F32)
            o_tile = acc_all[0:QB, 0:LANES]
            for e in range(HEADS_PER_LANE_TILE):
                h = t * HEADS_PER_LANE_TILE + e
                mine = acc_all[e * QB:(e + 1) * QB, :]
                stat_tile = jnp.where(lane == N_HEADS + h, mine[:, LANES:], stat_tile)
                if e:
                    o_tile = jnp.where(head_masks[e], mine[:, 0:LANES], o_tile)
            if flat_inputs:
                o_ref[g, :, cols] = o_tile.astype(o_ref.dtype)
            else:
                o_ref[at] = o_tile.reshape(4, 4, SUBLANES, LANES)
        stat_at = where + rows + chain
        stat_ref[stat_at] = stat_tile.reshape(4, 4, SUBLANES, LANES)


def _block_positions(dilation):
    row = np.arange(QB)
    if dilation == 16:
        return row
    if dilation == 4:
        return (row // 32) * 32 + (row % 8) * 4 + (row // 8) % 4
    if dilation == 1:
        return (row % 8) * 16 + row // 8
    raise NotImplementedError(dilation)


def _alibi_bias(dilation):
    pos = _block_positions(dilation)
    q_loc = pos[:, None] + QB
    k_loc = np.concatenate([pos, pos + QB])[None, :]
    delta = q_loc - k_loc
    valid = (delta >= 0) & (delta <= QB)
    has_prev = np.stack([np.ones((2 * QB,), bool), np.arange(2 * QB) >= QB])
    ok = valid[None, None] & has_prev[:, None, None, :]
    slopes = np.exp2(-8.0 * np.arange(1, N_HEADS + 1) / N_HEADS)
    bias = -slopes[:, None, None] * (dilation * delta)[None] * LOG2_E
    return jnp.asarray(np.where(ok, bias[None], -np.inf), F32)


def _box_spec(dilation, width, g, chains):
    if dilation == 16 and chains > 1:
        low = 4 // chains
        return pl.BlockSpec((None, g, 4, 4, None, chains, SUBLANES, width),
                            lambda b, r, i: (b, i, 0, 0, r // low, r % low, 0, 0))
    assert chains == 1
    if dilation == 16:
        return pl.BlockSpec((None, g, 4, 4, None, None, SUBLANES, width),
                            lambda b, r, i: (b, i, 0, 0, r // 4, r % 4, 0, 0))
    if g > 4:
        hi = g // 4
        if dilation == 4:
            return pl.BlockSpec((None, hi, 4, 4, 4, None, SUBLANES, width),
                                lambda b, r, i: (b, i, 0, 0, 0, r, 0, 0))
        if dilation == 1:
            per = 4 // hi
            return pl.BlockSpec((None, None, hi, 4, 4, 4, SUBLANES, width),
                                lambda b, r, i: (b, i // per, i % per, 0, 0, 0, 0, 0))
    per = 4 // g
    if dilation == 4:
        return pl.BlockSpec((None, None, g, 4, 4, None, SUBLANES, width),
                            lambda b, r, i: (b, i // per, i % per, 0, 0, r, 0, 0))
    if dilation == 1:
        return pl.BlockSpec((None, None, None, g, 4, 4, SUBLANES, width),
                            lambda b, r, i: (b, i // (4 * per), (i // per) % 4, i % per, 0, 0, 0, 0))
    raise NotImplementedError(dilation)


def _attn_stage(q, k, v, dilation, narrow=None):
    bsz, seq, d = q.shape
    nb = seq // (dilation * QB)
    view = lambda a: a.reshape(bsz, seq // 2048, 4, 4, 4, 4, SUBLANES, a.shape[-1])
    bias = _alibi_bias(dilation)
    bias_spec = pl.BlockSpec(bias.shape, lambda b, r, i: (0, 0, 0, 0),
                             pipeline_mode=pl.Buffered(1))
    g = min(ATTN_BLOCKS, nb)
    chains = min(ATTN_BLOCKS // g, 4) if dilation == 16 else 1
    lead = (g,) if g <= 4 or dilation == 16 else (g // 4, 4)
    box = _box_spec(dilation, d, g, chains)
    stat_box = _box_spec(dilation, LANES, g, chains)
    stat_shape = jax.ShapeDtypeStruct(view(q).shape[:-1] + (LANES,), F32)
    if narrow is None:
        in_box, operands = box, [view(q), view(k), view(v)]
        o_box, o_shape = box, jax.ShapeDtypeStruct(view(q).shape, F32)
    else:
        assert dilation == 1
        in_box = pl.BlockSpec((None, g, QB, d), lambda b, r, i: (b, i, 0, 0))
        operands = [a.reshape(bsz, seq // QB, QB, d) for a in narrow]
        o_box, o_shape = in_box, jax.ShapeDtypeStruct(operands[0].shape, BF16)
    o, stat = pl.pallas_call(
        functools.partial(_attn_kernel, lead=lead, chained=chains > 1),
        grid=(bsz, dilation // chains, nb // g),
        in_specs=[bias_spec, in_box, in_box, in_box],
        out_specs=[o_box, stat_box],
        out_shape=[o_shape, stat_shape],
        scratch_shapes=[pltpu.VMEM((chains * g, QB, d), BF16)] * 2,
        compiler_params=pltpu.CompilerParams(
            dimension_semantics=("arbitrary", "arbitrary", "arbitrary"),
            vmem_limit_bytes=VMEM_LIMIT_BYTES),
        name=f"attn_stage_d{dilation}",
    )(bias, *operands)
    return o.reshape(bsz * seq, d), stat.reshape(bsz * seq, LANES)


def _attn_pair_stage(q, k, v):
    bsz, seq, d = q.shape
    view = lambda a: a.reshape(bsz, seq // 2048, 4, 4, 4, 4, SUBLANES, a.shape[-1])
    box = lambda w: pl.BlockSpec((None, None, 4, 4, 4, None, SUBLANES, w),
                                 lambda b, r, a: (b, a, 0, 0, 0, r, 0, 0))
    biases = [_alibi_bias(4), _alibi_bias(16)]
    bias_spec = pl.BlockSpec(biases[0].shape, lambda b, r, a: (0, 0, 0, 0),
                             pipeline_mode=pl.Buffered(1))
    o_shape = jax.ShapeDtypeStruct(view(q).shape, F32)
    stat_shape = jax.ShapeDtypeStruct(view(q).shape[:-1] + (LANES,), F32)
    outs = pl.pallas_call(
        _attn_pair_kernel,
        grid=(bsz, 4, seq // 2048),
        in_specs=[bias_spec, bias_spec, box(d), box(d), box(d)],
        out_specs=[box(d), box(LANES)] * 2,
        out_shape=[o_shape, stat_shape] * 2,
        scratch_shapes=[pltpu.VMEM((4, QB, d), BF16)] * 4,
        compiler_params=pltpu.CompilerParams(
            dimension_semantics=("arbitrary", "arbitrary", "arbitrary"),
            vmem_limit_bytes=VMEM_LIMIT_BYTES),
        name="attn_stage_d4_d16",
    )(*biases, view(q), view(k), view(v))
    flat = [a.reshape(bsz * seq, a.shape[-1]) for a in outs]
    return (flat[0], flat[1]), (flat[2], flat[3])


def _out_kernel(*refs, n_patterns):
    x_ref, ac_ref, za_ref, gc_ref, ga_ref = refs[:5]
    o_refs = refs[5:5 + n_patterns]
    stat_refs = refs[5 + n_patterns:5 + 2 * n_patterns]
    spread_ref, perm_ref, woc_ref, woa_ref, wo_ref, fg_ref, y_ref = refs[5 + 2 * n_patterns:]

    ms = [r[...] for r in stat_refs]
    lane = lax.broadcasted_iota(jnp.int32, ms[0].shape, 1)
    ls = [jnp.where(lane < N_HEADS, pltpu.roll(m, LANES - N_HEADS, 1), 1.0) for m in ms]
    top = functools.reduce(jnp.maximum, ms)
    ws = [jnp.exp2(m - top) for m in ms]
    total = functools.reduce(jnp.add, [w * l for w, l in zip(ws, ls)])
    mixed_o = None
    for w, o_ref in zip(ws, o_refs):
        alpha = w / total
        hi = alpha.astype(BF16)
        lo = (alpha - hi.astype(F32)).astype(BF16)
        wide = jnp.dot(jnp.concatenate([hi, lo], axis=1), spread_ref[...],
                       preferred_element_type=F32)
        term = wide * o_ref[...].astype(F32)
        mixed_o = term if mixed_o is None else mixed_o + term
    gated = (za_ref[...].astype(F32) * mixed_o).astype(BF16)
    y_attn = jnp.dot(gated, woa_ref[...], preferred_element_type=F32)
    y_conv = jnp.dot(ac_ref[...], woc_ref[...], preferred_element_type=F32)
    merged = gc_ref[...].astype(F32) * y_conv + ga_ref[...].astype(F32) * y_attn
    merged = _permute_rows(perm_ref[...], merged.astype(BF16))
    for r in range(0, merged.shape[0], PERM_ROWS):
        chunk = slice(r, r + PERM_ROWS)
        mixed = jnp.dot(merged[chunk, :], wo_ref[...], preferred_element_type=F32)
        y_ref[chunk, :] = _rms_norm(x_ref[chunk, :] + mixed, fg_ref[...])


def _head_spread(d):
    s = np.zeros((2 * LANES, d), np.float32)
    for h in range(N_HEADS):
        s[h, h * HEAD_DIM:(h + 1) * HEAD_DIM] = 1.0
        s[LANES + h, h * HEAD_DIM:(h + 1) * HEAD_DIM] = 1.0
    return jnp.asarray(s, BF16)


def _out_stage(x, ac, za, gc, ga, os_, stats, w_out_conv, w_out_attn, w_o, final_g):
    n, d = x.shape
    rows = OUT_ROWS
    n_patterns = len(os_)
    tile = pl.BlockSpec((rows, d), lambda i: (i, 0))
    stat = pl.BlockSpec((rows, LANES), lambda i: (i, 0))
    const = lambda shape: pl.BlockSpec(shape, lambda i: (0,) * len(shape),
                                       pipeline_mode=pl.Buffered(1))
    return pl.pallas_call(
        functools.partial(_out_kernel, n_patterns=n_patterns),
        grid=(n // rows,),
        in_specs=([tile] * (5 + n_patterns) + [stat] * n_patterns
                  + [const((2 * LANES, d)), const((PERM_ROWS, PERM_ROWS))]
                  + [const((d, d))] * 3 + [const((1, d))]),
        out_specs=tile,
        out_shape=jax.ShapeDtypeStruct((n, d), F32),
        compiler_params=pltpu.CompilerParams(
            dimension_semantics=("arbitrary",),
            vmem_limit_bytes=VMEM_LIMIT_BYTES),
        name="out_stage",
    )(x, ac, za, gc, ga, *os_, *stats, _head_spread(d), _to_storage_order(PERM_ROWS).T,
      w_out_conv, w_out_attn, w_o, final_g)


def _layer(h, norm_g, w_in, b_merge, conv_w, w_out_conv, w_out_attn, w_o, out_g):
    bsz, seq, d = h.shape
    q, k, v, za, ac, gc, ga, qn, kn, vn = _proj_stage(
        h, norm_g[None, :], w_in.astype(BF16), b_merge[None, :], conv_w)
    os_, stats = zip(_attn_stage(q, k, v, 1, narrow=(qn, kn, vn)),
                     *_attn_pair_stage(q, k, v))
    flat = lambda a: a.reshape(bsz * seq, d)
    y = _out_stage(flat(h), flat(ac), flat(za), flat(gc), flat(ga), os_, stats,
                   w_out_conv.astype(BF16), w_out_attn.astype(BF16), w_o.astype(BF16),
                   out_g[None, :])
    return y.reshape(bsz, seq, d)


def kernel(x, norm_g, w_in, b_merge, conv_w, w_out_conv, w_out_attn, w_o, final_g):
    depth = norm_g.shape[0]
    assert depth == 1, "the fused output stage applies the final norm after a single layer"
    assert x.shape[-1] == N_HEADS * HEAD_DIM and x.shape[1] % 4096 == 0
    assert ATTN_PATTERNS == ((128, 1), (512, 4), (2048, 16)), "box specs are per pattern"
    return _layer(x, norm_g[0], w_in[0], b_merge[0], conv_w[0],
                  w_out_conv[0], w_out_attn[0], w_o[0], final_g)
```

```python
import functools

import jax
import jax.numpy as jnp
import numpy as np
from jax import lax
from jax.experimental import pallas as pl
from jax.experimental.pallas import tpu as pltpu

N_HEADS = 16
HEAD_DIM = 64
CONV_WIDTH = 3
ATTN_PATTERNS = ((128, 1), (512, 4), (2048, 16))
QB = 128
EPS = 1e-6
LOG2_E = 1.4426950408889634

LANES = 128
SUBLANES = 8
HEADS_PER_LANE_TILE = LANES // HEAD_DIM
GROUPS = QB // SUBLANES
ATTN_BLOCKS = 16
PERM_ROWS = 256
PROJ_ROWS = 256
OUT_ROWS = 512
VMEM_LIMIT_BYTES = 56 * 1024 * 1024

BF16 = jnp.bfloat16
F32 = jnp.float32


def _rms_norm(x, g):
    return x * lax.rsqrt(jnp.mean(x * x, axis=-1, keepdims=True) + EPS) * g


def _to_storage_order(rows):
    s = np.arange(rows)
    natural = (s // QB) * QB + (s % SUBLANES) * GROUPS + (s % QB) // SUBLANES
    p = np.zeros((rows, rows), np.float32)
    p[s, natural] = 1.0
    return jnp.asarray(p, BF16)


def _permute_rows(p, a):
    n = p.shape[0]
    return jnp.concatenate(
        [jnp.dot(p, a[i:i + n, :], preferred_element_type=F32).astype(BF16)
         for i in range(0, a.shape[0], n)], axis=0)


def _previous_token(a, carry_group):
    rows, d = a.shape
    first_row = lax.broadcasted_iota(jnp.int32, (SUBLANES, d), 0) == 0
    pieces = []
    prev_last = carry_group
    for blk in range(rows // QB):
        base = blk * QB
        last = a[base + QB - SUBLANES:base + QB, :]
        pieces.append(jnp.where(first_row, pltpu.roll(prev_last, 1, 0), pltpu.roll(last, 1, 0)))
        pieces.append(a[base:base + QB - SUBLANES, :])
        prev_last = last
    return jnp.concatenate(pieces, axis=0)


def _proj_kernel(x_ref, perm_ref, g_ref, w_ref, b_ref, cw_ref,
                 q_ref, k_ref, v_ref, za_ref, ac_ref, gc_ref, ga_ref,
                 qn_ref, kn_ref, vn_ref, carry):
    rows, d = x_ref.shape

    @pl.when(pl.program_id(1) == 0)
    def _():
        carry[...] = jnp.zeros(carry.shape, F32)

    for r in range(0, rows, PERM_ROWS):
        chunk = slice(r, r + PERM_ROWS)
        u = _permute_rows(perm_ref[...],
                          _rms_norm(x_ref[chunk, :], g_ref[...]).astype(BF16))

        def proj(c, u=u):
            return jnp.dot(u, w_ref[:, c * d:(c + 1) * d], preferred_element_type=F32)

        a = proj(2) * proj(0)
        back1 = _previous_token(a, carry[SUBLANES:2 * SUBLANES, :])
        back2 = _previous_token(back1, carry[0:SUBLANES, :])
        carry[...] = a[PERM_ROWS - 2 * SUBLANES:PERM_ROWS, :]
        c = cw_ref[2:3, :] * a + cw_ref[1:2, :] * back1 + cw_ref[0:1, :] * back2
        ac_ref[chunk, :] = (jax.nn.silu(proj(3)) * proj(1) * c).astype(BF16)

        za_ref[chunk, :] = jax.nn.silu(proj(7)).astype(BF16)
        gc_ref[chunk, :] = jax.nn.sigmoid(proj(8) + b_ref[:, 0:d]).astype(BF16)
        ga_ref[chunk, :] = jax.nn.sigmoid(proj(9) + b_ref[:, d:2 * d]).astype(BF16)
        for col, wide_ref, narrow_ref, scale in ((4, q_ref, qn_ref, HEAD_DIM ** -0.5 * LOG2_E),
                                                 (5, k_ref, kn_ref, None),
                                                 (6, v_ref, vn_ref, None)):
            val = proj(col) if scale is None else proj(col) * scale
            wide_ref[chunk, :] = val
            narrow_ref[chunk, :] = val.astype(BF16)


def _proj_stage(x, norm_g, w_in, b_merge, conv_w):
    bsz, seq, d = x.shape
    rows = PROJ_ROWS
    tile = pl.BlockSpec((None, rows, d), lambda b, i: (b, i, 0))
    const = lambda shape: pl.BlockSpec(shape, lambda b, i: (0,) * len(shape),
                                       pipeline_mode=pl.Buffered(1))
    wide = jax.ShapeDtypeStruct((bsz, seq, d), F32)
    narrow = jax.ShapeDtypeStruct((bsz, seq, d), BF16)
    return pl.pallas_call(
        _proj_kernel,
        grid=(bsz, seq // rows),
        in_specs=[tile, const((PERM_ROWS, PERM_ROWS)), const((1, d)), const(w_in.shape),
                  const((1, 2 * d)), const((CONV_WIDTH, d))],
        out_specs=[tile] * 10,
        out_shape=[wide] * 3 + [narrow] * 7,
        scratch_shapes=[pltpu.VMEM((2 * SUBLANES, d), F32)],
        compiler_params=pltpu.CompilerParams(
            dimension_semantics=("arbitrary", "arbitrary"),
            vmem_limit_bytes=VMEM_LIMIT_BYTES),
        name="proj_stage",
    )(x, _to_storage_order(PERM_ROWS), norm_g, w_in, b_merge, conv_w)


def _zero_first_predecessors(bufs, n_chains, n_blocks):
    for buf in bufs:
        for c in range(n_chains):
            buf[c * n_blocks] = jnp.zeros(buf.shape[1:], buf.dtype)


def _attn_kernel(bias_ref, q_ref, k_ref, v_ref, o_ref, stat_ref, kbuf, vbuf, *,
                 lead, chained):
    n_chains = o_ref.shape[-3] if chained else 1

    @pl.when(pl.program_id(2) == 0)
    def _():
        _zero_first_predecessors((kbuf, vbuf), n_chains, int(np.prod(lead)))

    _attend(bias_ref, q_ref, k_ref, v_ref, o_ref, stat_ref, kbuf, vbuf, lead, chained)


def _attn_pair_kernel(bias_a, bias_b, q_ref, k_ref, v_ref, o_a, stat_a, o_b, stat_b,
                      kbuf_a, vbuf_a, kbuf_b, vbuf_b):
    @pl.when(pl.program_id(2) == 0)
    def _():
        _zero_first_predecessors((kbuf_a, vbuf_a), 1, 4)
        _zero_first_predecessors((kbuf_b, vbuf_b), 4, 1)

    _attend(bias_a, q_ref, k_ref, v_ref, o_a, stat_a, kbuf_a, vbuf_a, (4,), False)
    _attend(bias_b, q_ref, k_ref, v_ref, o_b, stat_b, kbuf_b, vbuf_b, (), True)


def _attend(bias_ref, q_ref, k_ref, v_ref, o_ref, stat_ref, kbuf, vbuf, lead, chained):
    n_blocks, d = int(np.prod(lead)), q_ref.shape[-1]
    n_chains = o_ref.shape[-3] if chained else 1
    flat_inputs = len(q_ref.shape) == 3
    first = (pl.program_id(2) == 0).astype(jnp.int32)

    lane = lax.broadcasted_iota(jnp.int32, (QB, LANES), 1)
    head_masks = [(lane // HEAD_DIM) == e for e in range(HEADS_PER_LANE_TILE)]
    rows = (slice(None),) * 2

    for c, g in [(c, g) for c in range(n_chains) for g in range(n_blocks)]:
        chain = (c,) if chained else ()
        where = tuple(int(i) for i in np.unravel_index(g, lead))
        slot = c * n_blocks + g
        keep = c * n_blocks + (g + 1) % n_blocks
        stat_tile = jnp.zeros((QB, LANES), F32)
        for t in range(N_HEADS // HEADS_PER_LANE_TILE):
            cols = slice(t * LANES, (t + 1) * LANES)
            at = where + rows + chain + (slice(None), cols)
            if flat_inputs:
                block = lambda ref: ref[g, :, cols]
            else:
                block = lambda ref: ref[at].reshape(QB, LANES).astype(BF16)
            q, k_cur, v_cur = block(q_ref), block(k_ref), block(v_ref)
            if flat_inputs and g > 0:
                k_prev, v_prev = k_ref[g - 1, :, cols], v_ref[g - 1, :, cols]
            else:
                k_prev, v_prev = kbuf[slot, :, cols], vbuf[slot, :, cols]
            k = jnp.concatenate([k_prev, k_cur], axis=0)
            v = jnp.concatenate([v_prev, v_cur], axis=0)
            if not flat_inputs or g == n_blocks - 1:
                kbuf[keep, :, cols] = k_cur
                vbuf[keep, :, cols] = v_cur
            q_heads = jnp.concatenate(
                [jnp.where(m, q, jnp.zeros_like(q)) for m in head_masks], axis=0)
            s_all = lax.dot_general(q_heads, k, (((1,), (1,)), ((), ())),
                                    preferred_element_type=F32)
            ps = []
            for e in range(HEADS_PER_LANE_TILE):
                h = t * HEADS_PER_LANE_TILE + e
                s = s_all[e * QB:(e + 1) * QB, :] + bias_ref[first if g == 0 else 0, h]
                m = jnp.max(s, axis=-1, keepdims=True)
                p = jnp.exp2(s - m)
                stat_tile = jnp.where(lane == h, m, stat_tile)
                ps.append(p.astype(BF16))
            acc_all = jnp.dot(jnp.concatenate(ps, axis=0),
                              jnp.concatenate([v, jnp.ones_like(v)], axis=1),
                              preferred_element_type=F32)
            o_tile = acc_all[0:QB, 0:LANES]
            for e in range(HEADS_PER_LANE_TILE):
                h = t * HEADS_PER_LANE_TILE + e
                mine = acc_all[e * QB:(e + 1) * QB, :]
                stat_tile = jnp.where(lane == N_HEADS + h, mine[:, LANES:], stat_tile)
                if e:
                    o_tile = jnp.where(head_masks[e], mine[:, 0:LANES], o_tile)
            if flat_inputs:
                o_ref[g, :, cols] = o_tile.astype(o_ref.dtype)
            else:
                o_ref[at] = o_tile.reshape(4, 4, SUBLANES, LANES)
        stat_at = where + rows + chain
        stat_ref[stat_at] = stat_tile.reshape(4, 4, SUBLANES, LANES)


def _block_positions(dilation):
    row = np.arange(QB)
    if dilation == 16:
        return row
    if dilation == 4:
        return (row // 32) * 32 + (row % 8) * 4 + (row // 8) % 4
    if dilation == 1:
        return (row % 8) * 16 + row // 8
    raise NotImplementedError(dilation)


def _alibi_bias(dilation):
    pos = _block_positions(dilation)
    q_loc = pos[:, None] + QB
    k_loc = np.concatenate([pos, pos + QB])[None, :]
    delta = q_loc - k_loc
    valid = (delta >= 0) & (delta <= QB)
    has_prev = np.stack([np.ones((2 * QB,), bool), np.arange(2 * QB) >= QB])
    ok = valid[None, None] & has_prev[:, None, None, :]
    slopes = np.exp2(-8.0 * np.arange(1, N_HEADS + 1) / N_HEADS)
    bias = -slopes[:, None, None] * (dilation * delta)[None] * LOG2_E
    return jnp.asarray(np.where(ok, bias[None], -np.inf), F32)


def _box_spec(dilation, width, g, chains):
    if dilation == 16 and chains > 1:
        low = 4 // chains
        return pl.BlockSpec((None, g, 4, 4, None, chains, SUBLANES, width),
                            lambda b, r, i: (b, i, 0, 0, r // low, r % low, 0, 0))
    assert chains == 1
    if dilation == 16:
        return pl.BlockSpec((None, g, 4, 4, None, None, SUBLANES, width),
                            lambda b, r, i: (b, i, 0, 0, r // 4, r % 4, 0, 0))
    if g > 4:
        hi = g // 4
        if dilation == 4:
            return pl.BlockSpec((None, hi, 4, 4, 4, None, SUBLANES, width),
                                lambda b, r, i: (b, i, 0, 0, 0, r, 0, 0))
        if dilation == 1:
            per = 4 // hi
            return pl.BlockSpec((None, None, hi, 4, 4, 4, SUBLANES, width),
                                lambda b, r, i: (b, i // per, i % per, 0, 0, 0, 0, 0))
    per = 4 // g
    if dilation == 4:
        return pl.BlockSpec((None, None, g, 4, 4, None, SUBLANES, width),
                            lambda b, r, i: (b, i // per, i % per, 0, 0, r, 0, 0))
    if dilation == 1:
        return pl.BlockSpec((None, None, None, g, 4, 4, SUBLANES, width),
                            lambda b, r, i: (b, i // (4 * per), (i // per) % 4, i % per, 0, 0, 0, 0))
    raise NotImplementedError(dilation)


def _attn_stage(q, k, v, dilation, narrow=None):
    bsz, seq, d = q.shape
    nb = seq // (dilation * QB)
    view = lambda a: a.reshape(bsz, seq // 2048, 4, 4, 4, 4, SUBLANES, a.shape[-1])
    bias = _alibi_bias(dilation)
    bias_spec = pl.BlockSpec(bias.shape, lambda b, r, i: (0, 0, 0, 0),
                             pipeline_mode=pl.Buffered(1))
    g = min(ATTN_BLOCKS, nb)
    chains = min(ATTN_BLOCKS // g, 4) if dilation == 16 else 1
    lead = (g,) if g <= 4 or dilation == 16 else (g // 4, 4)
    box = _box_spec(dilation, d, g, chains)
    stat_box = _box_spec(dilation, LANES, g, chains)
    stat_shape = jax.ShapeDtypeStruct(view(q).shape[:-1] + (LANES,), F32)
    if narrow is None:
        in_box, operands = box, [view(q), view(k), view(v)]
        o_box, o_shape = box, jax.ShapeDtypeStruct(view(q).shape, F32)
    else:
        assert dilation == 1
        in_box = pl.BlockSpec((None, g, QB, d), lambda b, r, i: (b, i, 0, 0))
        operands = [a.reshape(bsz, seq // QB, QB, d) for a in narrow]
        o_box, o_shape = in_box, jax.ShapeDtypeStruct(operands[0].shape, BF16)
    o, stat = pl.pallas_call(
        functools.partial(_attn_kernel, lead=lead, chained=chains > 1),
        grid=(bsz, dilation // chains, nb // g),
        in_specs=[bias_spec, in_box, in_box, in_box],
        out_specs=[o_box, stat_box],
        out_shape=[o_shape, stat_shape],
        scratch_shapes=[pltpu.VMEM((chains * g if narrow is None else 1, QB, d), BF16)] * 2,
        compiler_params=pltpu.CompilerParams(
            dimension_semantics=("arbitrary", "arbitrary", "arbitrary"),
            vmem_limit_bytes=VMEM_LIMIT_BYTES),
        name=f"attn_stage_d{dilation}",
    )(bias, *operands)
    return o.reshape(bsz * seq, d), stat.reshape(bsz * seq, LANES)


def _attn_pair_stage(q, k, v):
    bsz, seq, d = q.shape
    view = lambda a: a.reshape(bsz, seq // 2048, 4, 4, 4, 4, SUBLANES, a.shape[-1])
    box = lambda w: pl.BlockSpec((None, None, 4, 4, 4, None, SUBLANES, w),
                                 lambda b, r, a: (b, a, 0, 0, 0, r, 0, 0))
    biases = [_alibi_bias(4), _alibi_bias(16)]
    bias_spec = pl.BlockSpec(biases[0].shape, lambda b, r, a: (0, 0, 0, 0),
                             pipeline_mode=pl.Buffered(1))
    o_shape = jax.ShapeDtypeStruct(view(q).shape, F32)
    stat_shape = jax.ShapeDtypeStruct(view(q).shape[:-1] + (LANES,), F32)
    outs = pl.pallas_call(
        _attn_pair_kernel,
        grid=(bsz, 4, seq // 2048),
        in_specs=[bias_spec, bias_spec, box(d), box(d), box(d)],
        out_specs=[box(d), box(LANES)] * 2,
        out_shape=[o_shape, stat_shape] * 2,
        scratch_shapes=[pltpu.VMEM((4, QB, d), BF16)] * 4,
        compiler_params=pltpu.CompilerParams(
            dimension_semantics=("arbitrary", "arbitrary", "arbitrary"),
            vmem_limit_bytes=VMEM_LIMIT_BYTES),
        name="attn_stage_d4_d16",
    )(*biases, view(q), view(k), view(v))
    flat = [a.reshape(bsz * seq, a.shape[-1]) for a in outs]
    return (flat[0], flat[1]), (flat[2], flat[3])


def _out_kernel(*refs, n_patterns):
    x_ref, ac_ref, za_ref, gc_ref, ga_ref = refs[:5]
    o_refs = refs[5:5 + n_patterns]
    stat_refs = refs[5 + n_patterns:5 + 2 * n_patterns]
    spread_ref, perm_ref, woc_ref, woa_ref, wo_ref, fg_ref, y_ref = refs[5 + 2 * n_patterns:]

    ms = [r[...] for r in stat_refs]
    lane = lax.broadcasted_iota(jnp.int32, ms[0].shape, 1)
    ls = [jnp.where(lane < N_HEADS, pltpu.roll(m, LANES - N_HEADS, 1), 1.0) for m in ms]
    top = functools.reduce(jnp.maximum, ms)
    ws = [jnp.exp2(m - top) for m in ms]
    total = functools.reduce(jnp.add, [w * l for w, l in zip(ws, ls)])
    mixed_o = None
    for w, o_ref in zip(ws, o_refs):
        alpha = w / total
        hi = alpha.astype(BF16)
        lo = (alpha - hi.astype(F32)).astype(BF16)
        wide = jnp.dot(jnp.concatenate([hi, lo], axis=1), spread_ref[...],
                       preferred_element_type=F32)
        term = wide * o_ref[...].astype(F32)
        mixed_o = term if mixed_o is None else mixed_o + term
    gated = (za_ref[...].astype(F32) * mixed_o).astype(BF16)
    y_attn = jnp.dot(gated, woa_ref[...], preferred_element_type=F32)
    y_conv = jnp.dot(ac_ref[...], woc_ref[...], preferred_element_type=F32)
    merged = gc_ref[...].astype(F32) * y_conv + ga_ref[...].astype(F32) * y_attn
    merged = _permute_rows(perm_ref[...], merged.astype(BF16))
    for r in range(0, merged.shape[0], PERM_ROWS):
        chunk = slice(r, r + PERM_ROWS)
        mixed = jnp.dot(merged[chunk, :], wo_ref[...], preferred_element_type=F32)
        y_ref[chunk, :] = _rms_norm(x_ref[chunk, :] + mixed, fg_ref[...])


def _head_spread(d):
    s = np.zeros((2 * LANES, d), np.float32)
    for h in range(N_HEADS):
        s[h, h * HEAD_DIM:(h + 1) * HEAD_DIM] = 1.0
        s[LANES + h, h * HEAD_DIM:(h + 1) * HEAD_DIM] = 1.0
    return jnp.asarray(s, BF16)


def _out_stage(x, ac, za, gc, ga, os_, stats, w_out_conv, w_out_attn, w_o, final_g):
    n, d = x.shape
    rows = OUT_ROWS
    n_patterns = len(os_)
    tile = pl.BlockSpec((rows, d), lambda i: (i, 0))
    stat = pl.BlockSpec((rows, LANES), lambda i: (i, 0))
    const = lambda shape: pl.BlockSpec(shape, lambda i: (0,) * len(shape),
                                       pipeline_mode=pl.Buffered(1))
    return pl.pallas_call(
        functools.partial(_out_kernel, n_patterns=n_patterns),
        grid=(n // rows,),
        in_specs=([tile] * (5 + n_patterns) + [stat] * n_patterns
                  + [const((2 * LANES, d)), const((PERM_ROWS, PERM_ROWS))]
                  + [const((d, d))] * 3 + [const((1, d))]),
        out_specs=tile,
        out_shape=jax.ShapeDtypeStruct((n, d), F32),
        compiler_params=pltpu.CompilerParams(
            dimension_semantics=("arbitrary",),
            vmem_limit_bytes=VMEM_LIMIT_BYTES),
        name="out_stage",
    )(x, ac, za, gc, ga, *os_, *stats, _head_spread(d), _to_storage_order(PERM_ROWS).T,
      w_out_conv, w_out_attn, w_o, final_g)


def _layer(h, norm_g, w_in, b_merge, conv_w, w_out_conv, w_out_attn, w_o, out_g):
    bsz, seq, d = h.shape
    q, k, v, za, ac, gc, ga, qn, kn, vn = _proj_stage(
        h, norm_g[None, :], w_in.astype(BF16), b_merge[None, :], conv_w)
    os_, stats = zip(_attn_stage(q, k, v, 1, narrow=(qn, kn, vn)),
                     *_attn_pair_stage(q, k, v))
    flat = lambda a: a.reshape(bsz * seq, d)
    y = _out_stage(flat(h), flat(ac), flat(za), flat(gc), flat(ga), os_, stats,
                   w_out_conv.astype(BF16), w_out_attn.astype(BF16), w_o.astype(BF16),
                   out_g[None, :])
    return y.reshape(bsz, seq, d)


def kernel(x, norm_g, w_in, b_merge, conv_w, w_out_conv, w_out_attn, w_o, final_g):
    depth = norm_g.shape[0]
    assert depth == 1, "the fused output stage applies the final norm after a single layer"
    assert x.shape[-1] == N_HEADS * HEAD_DIM and x.shape[1] % 4096 == 0
    assert ATTN_PATTERNS == ((128, 1), (512, 4), (2048, 16)), "box specs are per pattern"
    return _layer(x, norm_g[0], w_in[0], b_merge[0], conv_w[0],
                  w_out_conv[0], w_out_attn[0], w_o[0], final_g)
```

```python
import functools

import jax
import jax.numpy as jnp
import numpy as np
from jax import lax
from jax.experimental import pallas as pl
from jax.experimental.pallas import tpu as pltpu

N_HEADS = 16
HEAD_DIM = 64
CONV_WIDTH = 3
ATTN_PATTERNS = ((128, 1), (512, 4), (2048, 16))
QB = 128
EPS = 1e-6
LOG2_E = 1.4426950408889634

LANES = 128
SUBLANES = 8
HEADS_PER_LANE_TILE = LANES // HEAD_DIM
GROUPS = QB // SUBLANES
ATTN_BLOCKS = 16
PERM_ROWS = 256
PROJ_ROWS = 256
OUT_ROWS = 512
VMEM_LIMIT_BYTES = 56 * 1024 * 1024

BF16 = jnp.bfloat16
F32 = jnp.float32


def _rms_norm(x, g):
    return x * lax.rsqrt(jnp.mean(x * x, axis=-1, keepdims=True) + EPS) * g


def _to_storage_order(rows):
    s = np.arange(rows)
    natural = (s // QB) * QB + (s % SUBLANES) * GROUPS + (s % QB) // SUBLANES
    p = np.zeros((rows, rows), np.float32)
    p[s, natural] = 1.0
    return jnp.asarray(p, BF16)


def _permute_rows(p, a):
    n = p.shape[0]
    return jnp.concatenate(
        [jnp.dot(p, a[i:i + n, :], preferred_element_type=F32).astype(BF16)
         for i in range(0, a.shape[0], n)], axis=0)


def _previous_token(a, carry_group):
    rows, d = a.shape
    first_row = lax.broadcasted_iota(jnp.int32, (SUBLANES, d), 0) == 0
    pieces = []
    prev_last = carry_group
    for blk in range(rows // QB):
        base = blk * QB
        last = a[base + QB - SUBLANES:base + QB, :]
        pieces.append(jnp.where(first_row, pltpu.roll(prev_last, 1, 0), pltpu.roll(last, 1, 0)))
        pieces.append(a[base:base + QB - SUBLANES, :])
        prev_last = last
    return jnp.concatenate(pieces, axis=0)


def _proj_kernel(x_ref, perm_ref, g_ref, w_ref, b_ref, cw_ref,
                 q_ref, k_ref, v_ref, za_ref, ac_ref, gc_ref, ga_ref,
                 qn_ref, kn_ref, vn_ref, carry):
    rows, d = x_ref.shape

    @pl.when(pl.program_id(1) == 0)
    def _():
        carry[...] = jnp.zeros(carry.shape, F32)

    for r in range(0, rows, PERM_ROWS):
        chunk = slice(r, r + PERM_ROWS)
        u = _permute_rows(perm_ref[...],
                          _rms_norm(x_ref[chunk, :], g_ref[...]).astype(BF16))

        def proj(c, u=u):
            return jnp.dot(u, w_ref[:, c * d:(c + 1) * d], preferred_element_type=F32)

        a = proj(2) * proj(0)
        back1 = _previous_token(a, carry[SUBLANES:2 * SUBLANES, :])
        back2 = _previous_token(back1, carry[0:SUBLANES, :])
        carry[...] = a[PERM_ROWS - 2 * SUBLANES:PERM_ROWS, :]
        c = cw_ref[2:3, :] * a + cw_ref[1:2, :] * back1 + cw_ref[0:1, :] * back2
        ac_ref[chunk, :] = (jax.nn.silu(proj(3)) * proj(1) * c).astype(BF16)

        za_ref[chunk, :] = jax.nn.silu(proj(7)).astype(BF16)
        gc_ref[chunk, :] = jax.nn.sigmoid(proj(8) + b_ref[:, 0:d]).astype(BF16)
        ga_ref[chunk, :] = jax.nn.sigmoid(proj(9) + b_ref[:, d:2 * d]).astype(BF16)
        for col, wide_ref, narrow_ref, scale in ((4, q_ref, qn_ref, HEAD_DIM ** -0.5 * LOG2_E),
                                                 (5, k_ref, kn_ref, None),
                                                 (6, v_ref, vn_ref, None)):
            val = proj(col) if scale is None else proj(col) * scale
            wide_ref[chunk, :] = val
            narrow_ref[chunk, :] = val.astype(BF16)


def _proj_stage(x, norm_g, w_in, b_merge, conv_w):
    bsz, seq, d = x.shape
    rows = PROJ_ROWS
    tile = pl.BlockSpec((None, rows, d), lambda b, i: (b, i, 0))
    const = lambda shape: pl.BlockSpec(shape, lambda b, i: (0,) * len(shape),
                                       pipeline_mode=pl.Buffered(1))
    wide = jax.ShapeDtypeStruct((bsz, seq, d), F32)
    narrow = jax.ShapeDtypeStruct((bsz, seq, d), BF16)
    return pl.pallas_call(
        _proj_kernel,
        grid=(bsz, seq // rows),
        in_specs=[tile, const((PERM_ROWS, PERM_ROWS)), const((1, d)), const(w_in.shape),
                  const((1, 2 * d)), const((CONV_WIDTH, d))],
        out_specs=[tile] * 10,
        out_shape=[wide] * 3 + [narrow] * 7,
        scratch_shapes=[pltpu.VMEM((2 * SUBLANES, d), F32)],
        compiler_params=pltpu.CompilerParams(
            dimension_semantics=("arbitrary", "arbitrary"),
            vmem_limit_bytes=VMEM_LIMIT_BYTES),
        name="proj_stage",
    )(x, _to_storage_order(PERM_ROWS), norm_g, w_in, b_merge, conv_w)


def _zero_first_predecessors(bufs, n_chains, n_blocks):
    for buf in bufs:
        for c in range(n_chains):
            buf[c * n_blocks] = jnp.zeros(buf.shape[1:], buf.dtype)


def _attn_kernel(bias_ref, q_ref, k_ref, v_ref, o_ref, stat_ref, kbuf, vbuf, *,
                 lead, chained):
    n_chains = o_ref.shape[-3] if chained else 1

    @pl.when(pl.program_id(2) == 0)
    def _():
        _zero_first_predecessors((kbuf, vbuf), n_chains, int(np.prod(lead)))

    _attend(bias_ref, q_ref, k_ref, v_ref, o_ref, stat_ref, kbuf, vbuf, lead, chained)


def _attn_pair_kernel(bias_a, bias_b, q_ref, k_ref, v_ref, o_a, stat_a, o_b, stat_b,
                      kbuf_a, vbuf_a, kbuf_b, vbuf_b):
    @pl.when(pl.program_id(2) == 0)
    def _():
        _zero_first_predecessors((kbuf_a, vbuf_a), 1, 4)
        _zero_first_predecessors((kbuf_b, vbuf_b), 4, 1)

    _attend(bias_a, q_ref, k_ref, v_ref, o_a, stat_a, kbuf_a, vbuf_a, (4,), False)
    _attend(bias_b, q_ref, k_ref, v_ref, o_b, stat_b, kbuf_b, vbuf_b, (), True)


def _attend(bias_ref, q_ref, k_ref, v_ref, o_ref, stat_ref, kbuf, vbuf, lead, chained):
    n_blocks, d = int(np.prod(lead)), q_ref.shape[-1]
    n_chains = o_ref.shape[-3] if chained else 1
    flat_inputs = len(q_ref.shape) == 3
    first = (pl.program_id(2) == 0).astype(jnp.int32)

    lane = lax.broadcasted_iota(jnp.int32, (QB, LANES), 1)
    head_masks = [(lane // HEAD_DIM) == e for e in range(HEADS_PER_LANE_TILE)]
    rows = (slice(None),) * 2

    for c, g in [(c, g) for c in range(n_chains) for g in range(n_blocks)]:
        chain = (c,) if chained else ()
        where = tuple(int(i) for i in np.unravel_index(g, lead))
        slot = c * n_blocks + g
        keep = c * n_blocks + (g + 1) % n_blocks
        stat_tile = jnp.zeros((QB, LANES), F32)
        for t in range(N_HEADS // HEADS_PER_LANE_TILE):
            cols = slice(t * LANES, (t + 1) * LANES)
            at = where + rows + chain + (slice(None), cols)
            if flat_inputs:
                block = lambda ref: ref[g, :, cols]
            else:
                block = lambda ref: ref[at].reshape(QB, LANES).astype(BF16)
            q, k_cur, v_cur = block(q_ref), block(k_ref), block(v_ref)
            if flat_inputs and g > 0:
                k_prev, v_prev = k_ref[g - 1, :, cols], v_ref[g - 1, :, cols]
            else:
                k_prev, v_prev = kbuf[slot, :, cols], vbuf[slot, :, cols]
            k = jnp.concatenate([k_prev, k_cur], axis=0)
            v = jnp.concatenate([v_prev, v_cur], axis=0)
            if not flat_inputs or g == n_blocks - 1:
                kbuf[keep, :, cols] = k_cur
                vbuf[keep, :, cols] = v_cur
            q_heads = jnp.concatenate(
                [jnp.where(m, q, jnp.zeros_like(q)) for m in head_masks], axis=0)
            s_all = lax.dot_general(q_heads, k, (((1,), (1,)), ((), ())),
                                    preferred_element_type=F32)
            ps = []
            for e in range(HEADS_PER_LANE_TILE):
                h = t * HEADS_PER_LANE_TILE + e
                s = s_all[e * QB:(e + 1) * QB, :] + bias_ref[first if g == 0 else 0, h]
                m = jnp.max(s, axis=-1, keepdims=True)
                p = jnp.exp2(s - m)
                stat_tile = jnp.where(lane == h, m, stat_tile)
                ps.append(p.astype(BF16))
            acc_all = jnp.dot(jnp.concatenate(ps, axis=0),
                              jnp.concatenate([v, jnp.ones_like(v)], axis=1),
                              preferred_element_type=F32)
            o_tile = None
            for e in range(HEADS_PER_LANE_TILE):
                h = t * HEADS_PER_LANE_TILE + e
                mine = acc_all[e * QB:(e + 1) * QB, :]
                stat_tile = jnp.where(lane == N_HEADS + h, mine[:, LANES:], stat_tile)
                o_head = mine[:, 0:LANES] / mine[:, LANES:]
                o_tile = o_head if o_tile is None else jnp.where(head_masks[e], o_head, o_tile)
            if flat_inputs:
                o_ref[g, :, cols] = o_tile.astype(o_ref.dtype)
            else:
                o_ref[at] = o_tile.reshape(4, 4, SUBLANES, LANES)
        stat_at = where + rows + chain
        stat_ref[stat_at] = stat_tile.reshape(4, 4, SUBLANES, LANES)


def _block_positions(dilation):
    row = np.arange(QB)
    if dilation == 16:
        return row
    if dilation == 4:
        return (row // 32) * 32 + (row % 8) * 4 + (row // 8) % 4
    if dilation == 1:
        return (row % 8) * 16 + row // 8
    raise NotImplementedError(dilation)


def _alibi_bias(dilation):
    pos = _block_positions(dilation)
    q_loc = pos[:, None] + QB
    k_loc = np.concatenate([pos, pos + QB])[None, :]
    delta = q_loc - k_loc
    valid = (delta >= 0) & (delta <= QB)
    has_prev = np.stack([np.ones((2 * QB,), bool), np.arange(2 * QB) >= QB])
    ok = valid[None, None] & has_prev[:, None, None, :]
    slopes = np.exp2(-8.0 * np.arange(1, N_HEADS + 1) / N_HEADS)
    bias = -slopes[:, None, None] * (dilation * delta)[None] * LOG2_E
    return jnp.asarray(np.where(ok, bias[None], -np.inf), F32)


def _box_spec(dilation, width, g, chains):
    if dilation == 16 and chains > 1:
        low = 4 // chains
        return pl.BlockSpec((None, g, 4, 4, None, chains, SUBLANES, width),
                            lambda b, r, i: (b, i, 0, 0, r // low, r % low, 0, 0))
    assert chains == 1
    if dilation == 16:
        return pl.BlockSpec((None, g, 4, 4, None, None, SUBLANES, width),
                            lambda b, r, i: (b, i, 0, 0, r // 4, r % 4, 0, 0))
    if g > 4:
        hi = g // 4
        if dilation == 4:
            return pl.BlockSpec((None, hi, 4, 4, 4, None, SUBLANES, width),
                                lambda b, r, i: (b, i, 0, 0, 0, r, 0, 0))
        if dilation == 1:
            per = 4 // hi
            return pl.BlockSpec((None, None, hi, 4, 4, 4, SUBLANES, width),
                                lambda b, r, i: (b, i // per, i % per, 0, 0, 0, 0, 0))
    per = 4 // g
    if dilation == 4:
        return pl.BlockSpec((None, None, g, 4, 4, None, SUBLANES, width),
                            lambda b, r, i: (b, i // per, i % per, 0, 0, r, 0, 0))
    if dilation == 1:
        return pl.BlockSpec((None, None, None, g, 4, 4, SUBLANES, width),
                            lambda b, r, i: (b, i // (4 * per), (i // per) % 4, i % per, 0, 0, 0, 0))
    raise NotImplementedError(dilation)


def _attn_stage(q, k, v, dilation, narrow=None):
    bsz, seq, d = q.shape
    nb = seq // (dilation * QB)
    view = lambda a: a.reshape(bsz, seq // 2048, 4, 4, 4, 4, SUBLANES, a.shape[-1])
    bias = _alibi_bias(dilation)
    bias_spec = pl.BlockSpec(bias.shape, lambda b, r, i: (0, 0, 0, 0),
                             pipeline_mode=pl.Buffered(1))
    g = min(ATTN_BLOCKS, nb)
    chains = min(ATTN_BLOCKS // g, 4) if dilation == 16 else 1
    lead = (g,) if g <= 4 or dilation == 16 else (g // 4, 4)
    box = _box_spec(dilation, d, g, chains)
    stat_box = _box_spec(dilation, LANES, g, chains)
    stat_shape = jax.ShapeDtypeStruct(view(q).shape[:-1] + (LANES,), F32)
    if narrow is None:
        in_box, operands = box, [view(q), view(k), view(v)]
        o_box, o_shape = box, jax.ShapeDtypeStruct(view(q).shape, F32)
    else:
        assert dilation == 1
        in_box = pl.BlockSpec((None, g, QB, d), lambda b, r, i: (b, i, 0, 0))
        operands = [a.reshape(bsz, seq // QB, QB, d) for a in narrow]
        o_box, o_shape = in_box, jax.ShapeDtypeStruct(operands[0].shape, BF16)
    o, stat = pl.pallas_call(
        functools.partial(_attn_kernel, lead=lead, chained=chains > 1),
        grid=(bsz, dilation // chains, nb // g),
        in_specs=[bias_spec, in_box, in_box, in_box],
        out_specs=[o_box, stat_box],
        out_shape=[o_shape, stat_shape],
        scratch_shapes=[pltpu.VMEM((chains * g if narrow is None else 1, QB, d), BF16)] * 2,
        compiler_params=pltpu.CompilerParams(
            dimension_semantics=("arbitrary", "arbitrary", "arbitrary"),
            vmem_limit_bytes=VMEM_LIMIT_BYTES),
        name=f"attn_stage_d{dilation}",
    )(bias, *operands)
    return o.reshape(bsz * seq, d), stat.reshape(bsz * seq, LANES)


def _attn_pair_stage(q, k, v):
    bsz, seq, d = q.shape
    view = lambda a: a.reshape(bsz, seq // 2048, 4, 4, 4, 4, SUBLANES, a.shape[-1])
    box = lambda w: pl.BlockSpec((None, None, 4, 4, 4, None, SUBLANES, w),
                                 lambda b, r, a: (b, a, 0, 0, 0, r, 0, 0))
    biases = [_alibi_bias(4), _alibi_bias(16)]
    bias_spec = pl.BlockSpec(biases[0].shape, lambda b, r, a: (0, 0, 0, 0),
                             pipeline_mode=pl.Buffered(1))
    o_shape = jax.ShapeDtypeStruct(view(q).shape, F32)
    stat_shape = jax.ShapeDtypeStruct(view(q).shape[:-1] + (LANES,), F32)
    outs = pl.pallas_call(
        _attn_pair_kernel,
        grid=(bsz, 4, seq // 2048),
        in_specs=[bias_spec, bias_spec, box(d), box(d), box(d)],
        out_specs=[box(d), box(LANES)] * 2,
        out_shape=[o_shape, stat_shape] * 2,
        scratch_shapes=[pltpu.VMEM((4, QB, d), BF16)] * 4,
        compiler_params=pltpu.CompilerParams(
            dimension_semantics=("arbitrary", "arbitrary", "arbitrary"),
            vmem_limit_bytes=VMEM_LIMIT_BYTES),
        name="attn_stage_d4_d16",
    )(*biases, view(q), view(k), view(v))
    flat = [a.reshape(bsz * seq, a.shape[-1]) for a in outs]
    return (flat[0], flat[1]), (flat[2], flat[3])


def _out_kernel(*refs, n_patterns):
    x_ref, ac_ref, za_ref, gc_ref, ga_ref = refs[:5]
    o_refs = refs[5:5 + n_patterns]
    stat_refs = refs[5 + n_patterns:5 + 2 * n_patterns]
    spread_ref, perm_ref, woc_ref, woa_ref, wo_ref, fg_ref, y_ref = refs[5 + 2 * n_patterns:]

    ms = [r[...] for r in stat_refs]
    lane = lax.broadcasted_iota(jnp.int32, ms[0].shape, 1)
    ls = [jnp.where(lane < N_HEADS, pltpu.roll(m, LANES - N_HEADS, 1), 1.0) for m in ms]
    top = functools.reduce(jnp.maximum, ms)
    ws = [jnp.exp2(m - top) * l for m, l in zip(ms, ls)]
    total = functools.reduce(jnp.add, ws)
    mixed_o, rest = None, None
    for p, (w, o_ref) in enumerate(zip(ws, o_refs)):
        if p < n_patterns - 1:
            alpha = w / total
            hi = alpha.astype(BF16)
            lo = (alpha - hi.astype(F32)).astype(BF16)
            wide = jnp.dot(jnp.concatenate([hi, lo], axis=1), spread_ref[...],
                           preferred_element_type=F32)
            rest = 1.0 - wide if rest is None else rest - wide
        else:
            wide = rest
        term = wide * o_ref[...].astype(F32)
        mixed_o = term if mixed_o is None else mixed_o + term
    gated = (za_ref[...].astype(F32) * mixed_o).astype(BF16)
    y_attn = jnp.dot(gated, woa_ref[...], preferred_element_type=F32)
    y_conv = jnp.dot(ac_ref[...], woc_ref[...], preferred_element_type=F32)
    merged = gc_ref[...].astype(F32) * y_conv + ga_ref[...].astype(F32) * y_attn
    merged = _permute_rows(perm_ref[...], merged.astype(BF16))
    for r in range(0, merged.shape[0], PERM_ROWS):
        chunk = slice(r, r + PERM_ROWS)
        mixed = jnp.dot(merged[chunk, :], wo_ref[...], preferred_element_type=F32)
        y_ref[chunk, :] = _rms_norm(x_ref[chunk, :] + mixed, fg_ref[...])


def _head_spread(d):
    s = np.zeros((2 * LANES, d), np.float32)
    for h in range(N_HEADS):
        s[h, h * HEAD_DIM:(h + 1) * HEAD_DIM] = 1.0
        s[LANES + h, h * HEAD_DIM:(h + 1) * HEAD_DIM] = 1.0
    return jnp.asarray(s, BF16)


def _out_stage(x, ac, za, gc, ga, os_, stats, w_out_conv, w_out_attn, w_o, final_g):
    n, d = x.shape
    rows = OUT_ROWS
    n_patterns = len(os_)
    tile = pl.BlockSpec((rows, d), lambda i: (i, 0))
    stat = pl.BlockSpec((rows, LANES), lambda i: (i, 0))
    const = lambda shape: pl.BlockSpec(shape, lambda i: (0,) * len(shape),
                                       pipeline_mode=pl.Buffered(1))
    return pl.pallas_call(
        functools.partial(_out_kernel, n_patterns=n_patterns),
        grid=(n // rows,),
        in_specs=([tile] * (5 + n_patterns) + [stat] * n_patterns
                  + [const((2 * LANES, d)), const((PERM_ROWS, PERM_ROWS))]
                  + [const((d, d))] * 3 + [const((1, d))]),
        out_specs=tile,
        out_shape=jax.ShapeDtypeStruct((n, d), F32),
        compiler_params=pltpu.CompilerParams(
            dimension_semantics=("arbitrary",),
            vmem_limit_bytes=VMEM_LIMIT_BYTES),
        name="out_stage",
    )(x, ac, za, gc, ga, *os_, *stats, _head_spread(d), _to_storage_order(PERM_ROWS).T,
      w_out_conv, w_out_attn, w_o, final_g)


def _layer(h, norm_g, w_in, b_merge, conv_w, w_out_conv, w_out_attn, w_o, out_g):
    bsz, seq, d = h.shape
    q, k, v, za, ac, gc, ga, qn, kn, vn = _proj_stage(
        h, norm_g[None, :], w_in.astype(BF16), b_merge[None, :], conv_w)
    os_, stats = zip(_attn_stage(q, k, v, 1, narrow=(qn, kn, vn)),
                     *_attn_pair_stage(q, k, v))
    flat = lambda a: a.reshape(bsz * seq, d)
    y = _out_stage(flat(h), flat(ac), flat(za), flat(gc), flat(ga), os_, stats,
                   w_out_conv.astype(BF16), w_out_attn.astype(BF16), w_o.astype(BF16),
                   out_g[None, :])
    return y.reshape(bsz, seq, d)


def kernel(x, norm_g, w_in, b_merge, conv_w, w_out_conv, w_out_attn, w_o, final_g):
    depth = norm_g.shape[0]
    assert depth == 1, "the fused output stage applies the final norm after a single layer"
    assert x.shape[-1] == N_HEADS * HEAD_DIM and x.shape[1] % 4096 == 0
    assert ATTN_PATTERNS == ((128, 1), (512, 4), (2048, 16)), "box specs are per pattern"
    return _layer(x, norm_g[0], w_in[0], b_merge[0], conv_w[0],
                  w_out_conv[0], w_out_attn[0], w_o[0], final_g)
```

```python
import functools

import jax
import jax.numpy as jnp
import numpy as np
from jax import lax
from jax.experimental import pallas as pl
from jax.experimental.pallas import tpu as pltpu

N_HEADS = 16
HEAD_DIM = 64
CONV_WIDTH = 3
ATTN_PATTERNS = ((128, 1), (512, 4), (2048, 16))
QB = 128
EPS = 1e-6
LOG2_E = 1.4426950408889634

LANES = 128
SUBLANES = 8
HEADS_PER_LANE_TILE = LANES // HEAD_DIM
GROUPS = QB // SUBLANES
ATTN_BLOCKS = 16
PERM_ROWS = 256
PROJ_ROWS = 256
OUT_ROWS = 512
VMEM_LIMIT_BYTES = 56 * 1024 * 1024

BF16 = jnp.bfloat16
F32 = jnp.float32


def _rms_norm(x, g):
    return x * lax.rsqrt(jnp.mean(x * x, axis=-1, keepdims=True) + EPS) * g


def _to_storage_order(rows):
    s = np.arange(rows)
    natural = (s // QB) * QB + (s % SUBLANES) * GROUPS + (s % QB) // SUBLANES
    p = np.zeros((rows, rows), np.float32)
    p[s, natural] = 1.0
    return jnp.asarray(p, BF16)


def _permute_rows(p, a):
    n = p.shape[0]
    return jnp.concatenate(
        [jnp.dot(p, a[i:i + n, :], preferred_element_type=F32).astype(BF16)
         for i in range(0, a.shape[0], n)], axis=0)


def _previous_token(a, carry_group):
    rows, d = a.shape
    first_row = lax.broadcasted_iota(jnp.int32, (SUBLANES, d), 0) == 0
    pieces = []
    prev_last = carry_group
    for blk in range(rows // QB):
        base = blk * QB
        last = a[base + QB - SUBLANES:base + QB, :]
        pieces.append(jnp.where(first_row, pltpu.roll(prev_last, 1, 0), pltpu.roll(last, 1, 0)))
        pieces.append(a[base:base + QB - SUBLANES, :])
        prev_last = last
    return jnp.concatenate(pieces, axis=0)


def _proj_kernel(x_ref, perm_ref, g_ref, w_ref, b_ref, cw_ref,
                 q_ref, k_ref, v_ref, za_ref, ac_ref, gc_ref, ga_ref,
                 qn_ref, kn_ref, vn_ref, carry):
    rows, d = x_ref.shape

    @pl.when(pl.program_id(1) == 0)
    def _():
        carry[...] = jnp.zeros(carry.shape, F32)

    for r in range(0, rows, PERM_ROWS):
        chunk = slice(r, r + PERM_ROWS)
        u = _permute_rows(perm_ref[...],
                          _rms_norm(x_ref[chunk, :], g_ref[...]).astype(BF16))

        def proj(c, u=u):
            return jnp.dot(u, w_ref[:, c * d:(c + 1) * d], preferred_element_type=F32)

        a = proj(2) * proj(0)
        back1 = _previous_token(a, carry[SUBLANES:2 * SUBLANES, :])
        back2 = _previous_token(back1, carry[0:SUBLANES, :])
        carry[...] = a[PERM_ROWS - 2 * SUBLANES:PERM_ROWS, :]
        c = cw_ref[2:3, :] * a + cw_ref[1:2, :] * back1 + cw_ref[0:1, :] * back2
        ac_ref[chunk, :] = (jax.nn.silu(proj(3)) * proj(1) * c).astype(BF16)

        za_ref[chunk, :] = jax.nn.silu(proj(7)).astype(BF16)
        gc_ref[chunk, :] = jax.nn.sigmoid(proj(8) + b_ref[:, 0:d]).astype(BF16)
        ga_ref[chunk, :] = jax.nn.sigmoid(proj(9) + b_ref[:, d:2 * d]).astype(BF16)
        for col, wide_ref, narrow_ref, scale in ((4, q_ref, qn_ref, HEAD_DIM ** -0.5 * LOG2_E),
                                                 (5, k_ref, kn_ref, None),
                                                 (6, v_ref, vn_ref, None)):
            val = proj(col) if scale is None else proj(col) * scale
            wide_ref[chunk, :] = val
            narrow_ref[chunk, :] = val.astype(BF16)


def _proj_stage(x, norm_g, w_in, b_merge, conv_w):
    bsz, seq, d = x.shape
    rows = PROJ_ROWS
    tile = pl.BlockSpec((None, rows, d), lambda b, i: (b, i, 0))
    const = lambda shape: pl.BlockSpec(shape, lambda b, i: (0,) * len(shape),
                                       pipeline_mode=pl.Buffered(1))
    wide = jax.ShapeDtypeStruct((bsz, seq, d), F32)
    narrow = jax.ShapeDtypeStruct((bsz, seq, d), BF16)
    return pl.pallas_call(
        _proj_kernel,
        grid=(bsz, seq // rows),
        in_specs=[tile, const((PERM_ROWS, PERM_ROWS)), const((1, d)), const(w_in.shape),
                  const((1, 2 * d)), const((CONV_WIDTH, d))],
        out_specs=[tile] * 10,
        out_shape=[wide] * 3 + [narrow] * 7,
        scratch_shapes=[pltpu.VMEM((2 * SUBLANES, d), F32)],
        compiler_params=pltpu.CompilerParams(
            dimension_semantics=("arbitrary", "arbitrary"),
            vmem_limit_bytes=VMEM_LIMIT_BYTES),
        name="proj_stage",
    )(x, _to_storage_order(PERM_ROWS), norm_g, w_in, b_merge, conv_w)


def _zero_first_predecessors(bufs, n_chains, n_blocks):
    for buf in bufs:
        for c in range(n_chains):
            buf[c * n_blocks] = jnp.zeros(buf.shape[1:], buf.dtype)


def _attn_kernel(bias_ref, q_ref, k_ref, v_ref, o_ref, stat_ref, kbuf, vbuf, *,
                 lead, chained):
    n_chains = o_ref.shape[-3] if chained else 1

    @pl.when(pl.program_id(2) == 0)
    def _():
        _zero_first_predecessors((kbuf, vbuf), n_chains, int(np.prod(lead)))

    _attend(bias_ref, q_ref, k_ref, v_ref, o_ref, stat_ref, kbuf, vbuf, lead, chained)


def _attn_pair_kernel(bias_a, bias_b, q_ref, k_ref, v_ref, o_a, stat_a, o_b, stat_b,
                      kbuf_a, vbuf_a, kbuf_b, vbuf_b):
    @pl.when(pl.program_id(2) == 0)
    def _():
        _zero_first_predecessors((kbuf_a, vbuf_a), 1, 4)
        _zero_first_predecessors((kbuf_b, vbuf_b), 4, 1)

    _attend(bias_a, q_ref, k_ref, v_ref, o_a, stat_a, kbuf_a, vbuf_a, (4,), False)
    _attend(bias_b, q_ref, k_ref, v_ref, o_b, stat_b, kbuf_b, vbuf_b, (), True)


def _attend(bias_ref, q_ref, k_ref, v_ref, o_ref, stat_ref, kbuf, vbuf, lead, chained):
    n_blocks, d = int(np.prod(lead)), q_ref.shape[-1]
    n_chains = o_ref.shape[-3] if chained else 1
    flat_inputs = len(q_ref.shape) == 3
    first = (pl.program_id(2) == 0).astype(jnp.int32)

    lane = lax.broadcasted_iota(jnp.int32, (QB, LANES), 1)
    head_masks = [(lane // HEAD_DIM) == e for e in range(HEADS_PER_LANE_TILE)]
    rows = (slice(None),) * 2

    for c, g in [(c, g) for c in range(n_chains) for g in range(n_blocks)]:
        chain = (c,) if chained else ()
        where = tuple(int(i) for i in np.unravel_index(g, lead))
        slot = c * n_blocks + g
        keep = c * n_blocks + (g + 1) % n_blocks
        stat_tile = jnp.zeros((QB, LANES), F32)
        for t in range(N_HEADS // HEADS_PER_LANE_TILE):
            cols = slice(t * LANES, (t + 1) * LANES)
            at = where + rows + chain + (slice(None), cols)
            if flat_inputs:
                block = lambda ref: ref[g, :, cols]
            else:
                block = lambda ref: ref[at].reshape(QB, LANES).astype(BF16)
            q, k_cur, v_cur = block(q_ref), block(k_ref), block(v_ref)
            if flat_inputs and g > 0:
                k_prev, v_prev = k_ref[g - 1, :, cols], v_ref[g - 1, :, cols]
            elif g > 0:
                before = tuple(int(i) for i in np.unravel_index(g - 1, lead))
                at_prev = before + rows + chain + (slice(None), cols)
                k_prev = k_ref[at_prev].reshape(QB, LANES).astype(BF16)
                v_prev = v_ref[at_prev].reshape(QB, LANES).astype(BF16)
            else:
                k_prev, v_prev = kbuf[slot, :, cols], vbuf[slot, :, cols]
            k = jnp.concatenate([k_prev, k_cur], axis=0)
            v = jnp.concatenate([v_prev, v_cur], axis=0)
            if g == n_blocks - 1:
                kbuf[keep, :, cols] = k_cur
                vbuf[keep, :, cols] = v_cur
            q_heads = jnp.concatenate(
                [jnp.where(m, q, jnp.zeros_like(q)) for m in head_masks], axis=0)
            s_all = lax.dot_general(q_heads, k, (((1,), (1,)), ((), ())),
                                    preferred_element_type=F32)
            ps = []
            for e in range(HEADS_PER_LANE_TILE):
                h = t * HEADS_PER_LANE_TILE + e
                s = s_all[e * QB:(e + 1) * QB, :] + bias_ref[first if g == 0 else 0, h]
                m = jnp.max(s, axis=-1, keepdims=True)
                p = jnp.exp2(s - m)
                stat_tile = jnp.where(lane == h, m, stat_tile)
                ps.append(p.astype(BF16))
            acc_all = jnp.dot(jnp.concatenate(ps, axis=0),
                              jnp.concatenate([v, jnp.ones_like(v)], axis=1),
                              preferred_element_type=F32)
            o_tile = acc_all[0:QB, 0:LANES]
            for e in range(HEADS_PER_LANE_TILE):
                h = t * HEADS_PER_LANE_TILE + e
                mine = acc_all[e * QB:(e + 1) * QB, :]
                stat_tile = jnp.where(lane == N_HEADS + h, mine[:, LANES:], stat_tile)
                if e:
                    o_tile = jnp.where(head_masks[e], mine[:, 0:LANES], o_tile)
            if flat_inputs:
                o_ref[g, :, cols] = o_tile.astype(o_ref.dtype)
            else:
                o_ref[at] = o_tile.reshape(4, 4, SUBLANES, LANES)
        stat_at = where + rows + chain
        stat_ref[stat_at] = stat_tile.reshape(4, 4, SUBLANES, LANES)


def _block_positions(dilation):
    row = np.arange(QB)
    if dilation == 16:
        return row
    if dilation == 4:
        return (row // 32) * 32 + (row % 8) * 4 + (row // 8) % 4
    if dilation == 1:
        return (row % 8) * 16 + row // 8
    raise NotImplementedError(dilation)


def _alibi_bias(dilation):
    pos = _block_positions(dilation)
    q_loc = pos[:, None] + QB
    k_loc = np.concatenate([pos, pos + QB])[None, :]
    delta = q_loc - k_loc
    valid = (delta >= 0) & (delta <= QB)
    has_prev = np.stack([np.ones((2 * QB,), bool), np.arange(2 * QB) >= QB])
    ok = valid[None, None] & has_prev[:, None, None, :]
    slopes = np.exp2(-8.0 * np.arange(1, N_HEADS + 1) / N_HEADS)
    bias = -slopes[:, None, None] * (dilation * delta)[None] * LOG2_E
    return jnp.asarray(np.where(ok, bias[None], -np.inf), F32)


def _box_spec(dilation, width, g, chains):
    if dilation == 16 and chains > 1:
        low = 4 // chains
        return pl.BlockSpec((None, g, 4, 4, None, chains, SUBLANES, width),
                            lambda b, r, i: (b, i, 0, 0, r // low, r % low, 0, 0))
    assert chains == 1
    if dilation == 16:
        return pl.BlockSpec((None, g, 4, 4, None, None, SUBLANES, width),
                            lambda b, r, i: (b, i, 0, 0, r // 4, r % 4, 0, 0))
    if g > 4:
        hi = g // 4
        if dilation == 4:
            return pl.BlockSpec((None, hi, 4, 4, 4, None, SUBLANES, width),
                                lambda b, r, i: (b, i, 0, 0, 0, r, 0, 0))
        if dilation == 1:
            per = 4 // hi
            return pl.BlockSpec((None, None, hi, 4, 4, 4, SUBLANES, width),
                                lambda b, r, i: (b, i // per, i % per, 0, 0, 0, 0, 0))
    per = 4 // g
    if dilation == 4:
        return pl.BlockSpec((None, None, g, 4, 4, None, SUBLANES, width),
                            lambda b, r, i: (b, i // per, i % per, 0, 0, r, 0, 0))
    if dilation == 1:
        return pl.BlockSpec((None, None, None, g, 4, 4, SUBLANES, width),
                            lambda b, r, i: (b, i // (4 * per), (i // per) % 4, i % per, 0, 0, 0, 0))
    raise NotImplementedError(dilation)


def _attn_stage(q, k, v, dilation, narrow=None):
    bsz, seq, d = q.shape
    nb = seq // (dilation * QB)
    view = lambda a: a.reshape(bsz, seq // 2048, 4, 4, 4, 4, SUBLANES, a.shape[-1])
    bias = _alibi_bias(dilation)
    bias_spec = pl.BlockSpec(bias.shape, lambda b, r, i: (0, 0, 0, 0),
                             pipeline_mode=pl.Buffered(1))
    g = min(ATTN_BLOCKS, nb)
    chains = min(ATTN_BLOCKS // g, 4) if dilation == 16 else 1
    lead = (g,) if g <= 4 or dilation == 16 else (g // 4, 4)
    box = _box_spec(dilation, d, g, chains)
    stat_box = _box_spec(dilation, LANES, g, chains)
    stat_shape = jax.ShapeDtypeStruct(view(q).shape[:-1] + (LANES,), F32)
    if narrow is None:
        in_box, operands = box, [view(q), view(k), view(v)]
        o_box, o_shape = box, jax.ShapeDtypeStruct(view(q).shape, F32)
    else:
        assert dilation == 1
        in_box = pl.BlockSpec((None, g, QB, d), lambda b, r, i: (b, i, 0, 0))
        operands = [a.reshape(bsz, seq // QB, QB, d) for a in narrow]
        o_box, o_shape = in_box, jax.ShapeDtypeStruct(operands[0].shape, BF16)
    o, stat = pl.pallas_call(
        functools.partial(_attn_kernel, lead=lead, chained=chains > 1),
        grid=(bsz, dilation // chains, nb // g),
        in_specs=[bias_spec, in_box, in_box, in_box],
        out_specs=[o_box, stat_box],
        out_shape=[o_shape, stat_shape],
        scratch_shapes=[pltpu.VMEM((chains * g if narrow is None else 1, QB, d), BF16)] * 2,
        compiler_params=pltpu.CompilerParams(
            dimension_semantics=("arbitrary", "arbitrary", "arbitrary"),
            vmem_limit_bytes=VMEM_LIMIT_BYTES),
        name=f"attn_stage_d{dilation}",
    )(bias, *operands)
    return o.reshape(bsz * seq, d), stat.reshape(bsz * seq, LANES)


def _attn_pair_stage(q, k, v):
    bsz, seq, d = q.shape
    view = lambda a: a.reshape(bsz, seq // 2048, 4, 4, 4, 4, SUBLANES, a.shape[-1])
    box = lambda w: pl.BlockSpec((None, None, 4, 4, 4, None, SUBLANES, w),
                                 lambda b, r, a: (b, a, 0, 0, 0, r, 0, 0))
    biases = [_alibi_bias(4), _alibi_bias(16)]
    bias_spec = pl.BlockSpec(biases[0].shape, lambda b, r, a: (0, 0, 0, 0),
                             pipeline_mode=pl.Buffered(1))
    o_shape = jax.ShapeDtypeStruct(view(q).shape, F32)
    stat_shape = jax.ShapeDtypeStruct(view(q).shape[:-1] + (LANES,), F32)
    outs = pl.pallas_call(
        _attn_pair_kernel,
        grid=(bsz, 4, seq // 2048),
        in_specs=[bias_spec, bias_spec, box(d), box(d), box(d)],
        out_specs=[box(d), box(LANES)] * 2,
        out_shape=[o_shape, stat_shape] * 2,
        scratch_shapes=[pltpu.VMEM((4, QB, d), BF16)] * 4,
        compiler_params=pltpu.CompilerParams(
            dimension_semantics=("arbitrary", "arbitrary", "arbitrary"),
            vmem_limit_bytes=VMEM_LIMIT_BYTES),
        name="attn_stage_d4_d16",
    )(*biases, view(q), view(k), view(v))
    flat = [a.reshape(bsz * seq, a.shape[-1]) for a in outs]
    return (flat[0], flat[1]), (flat[2], flat[3])


def _out_kernel(*refs, n_patterns):
    x_ref, ac_ref, za_ref, gc_ref, ga_ref = refs[:5]
    o_refs = refs[5:5 + n_patterns]
    stat_refs = refs[5 + n_patterns:5 + 2 * n_patterns]
    spread_ref, perm_ref, woc_ref, woa_ref, wo_ref, fg_ref, y_ref = refs[5 + 2 * n_patterns:]

    ms = [r[...] for r in stat_refs]
    lane = lax.broadcasted_iota(jnp.int32, ms[0].shape, 1)
    ls = [jnp.where(lane < N_HEADS, pltpu.roll(m, LANES - N_HEADS, 1), 1.0) for m in ms]
    top = functools.reduce(jnp.maximum, ms)
    ws = [jnp.exp2(m - top) for m in ms]
    total = functools.reduce(jnp.add, [w * l for w, l in zip(ws, ls)])
    mixed_o = None
    for w, o_ref in zip(ws, o_refs):
        alpha = w / total
        hi = alpha.astype(BF16)
        lo = (alpha - hi.astype(F32)).astype(BF16)
        wide = jnp.dot(jnp.concatenate([hi, lo], axis=1), spread_ref[...],
                       preferred_element_type=F32)
        term = wide * o_ref[...].astype(F32)
        mixed_o = term if mixed_o is None else mixed_o + term
    gated = (za_ref[...].astype(F32) * mixed_o).astype(BF16)
    y_attn = jnp.dot(gated, woa_ref[...], preferred_element_type=F32)
    y_conv = jnp.dot(ac_ref[...], woc_ref[...], preferred_element_type=F32)
    merged = gc_ref[...].astype(F32) * y_conv + ga_ref[...].astype(F32) * y_attn
    merged = _permute_rows(perm_ref[...], merged.astype(BF16))
    for r in range(0, merged.shape[0], PERM_ROWS):
        chunk = slice(r, r + PERM_ROWS)
        mixed = jnp.dot(merged[chunk, :], wo_ref[...], preferred_element_type=F32)
        y_ref[chunk, :] = _rms_norm(x_ref[chunk, :] + mixed, fg_ref[...])


def _head_spread(d):
    s = np.zeros((2 * LANES, d), np.float32)
    for h in range(N_HEADS):
        s[h, h * HEAD_DIM:(h + 1) * HEAD_DIM] = 1.0
        s[LANES + h, h * HEAD_DIM:(h + 1) * HEAD_DIM] = 1.0
    return jnp.asarray(s, BF16)


def _out_stage(x, ac, za, gc, ga, os_, stats, w_out_conv, w_out_attn, w_o, final_g):
    n, d = x.shape
    rows = OUT_ROWS
    n_patterns = len(os_)
    tile = pl.BlockSpec((rows, d), lambda i: (i, 0))
    stat = pl.BlockSpec((rows, LANES), lambda i: (i, 0))
    const = lambda shape: pl.BlockSpec(shape, lambda i: (0,) * len(shape),
                                       pipeline_mode=pl.Buffered(1))
    return pl.pallas_call(
        functools.partial(_out_kernel, n_patterns=n_patterns),
        grid=(n // rows,),
        in_specs=([tile] * (5 + n_patterns) + [stat] * n_patterns
                  + [const((2 * LANES, d)), const((PERM_ROWS, PERM_ROWS))]
                  + [const((d, d))] * 3 + [const((1, d))]),
        out_specs=tile,
        out_shape=jax.ShapeDtypeStruct((n, d), F32),
        compiler_params=pltpu.CompilerParams(
            dimension_semantics=("arbitrary",),
            vmem_limit_bytes=VMEM_LIMIT_BYTES),
        name="out_stage",
    )(x, ac, za, gc, ga, *os_, *stats, _head_spread(d), _to_storage_order(PERM_ROWS).T,
      w_out_conv, w_out_attn, w_o, final_g)


def _layer(h, norm_g, w_in, b_merge, conv_w, w_out_conv, w_out_attn, w_o, out_g):
    bsz, seq, d = h.shape
    q, k, v, za, ac, gc, ga, qn, kn, vn = _proj_stage(
        h, norm_g[None, :], w_in.astype(BF16), b_merge[None, :], conv_w)
    os_, stats = zip(_attn_stage(q, k, v, 1, narrow=(qn, kn, vn)),
                     *_attn_pair_stage(q, k, v))
    flat = lambda a: a.reshape(bsz * seq, d)
    y = _out_stage(flat(h), flat(ac), flat(za), flat(gc), flat(ga), os_, stats,
                   w_out_conv.astype(BF16), w_out_attn.astype(BF16), w_o.astype(BF16),
                   out_g[None, :])
    return y.reshape(bsz, seq, d)


def kernel(x, norm_g, w_in, b_merge, conv_w, w_out_conv, w_out_attn, w_o, final_g):
    depth = norm_g.shape[0]
    assert depth == 1, "the fused output stage applies the final norm after a single layer"
    assert x.shape[-1] == N_HEADS * HEAD_DIM and x.shape[1] % 4096 == 0
    assert ATTN_PATTERNS == ((128, 1), (512, 4), (2048, 16)), "box specs are per pattern"
    return _layer(x, norm_g[0], w_in[0], b_merge[0], conv_w[0],
                  w_out_conv[0], w_out_attn[0], w_o[0], final_g)
```
